```python
import jax
import jax.numpy as jnp
from jax import lax
import numpy as np

D_MODEL = 2048
BATCH = 4
SEQ = 4096
DEPTH = 2

N_MIXERS = 4
GROUP_WIDTH = D_MODEL // N_MIXERS
MIX_WIDTH = N_MIXERS * GROUP_WIDTH
HEAD_DIM = 128
N_HEADS = GROUP_WIDTH // HEAD_DIM
LRU_BLOCKS = N_HEADS
LRU_BLOCK_WIDTH = GROUP_WIDTH // LRU_BLOCKS
LRU_CONV_WIDTH = 4
LRU_C = 8.0
MLSTM_CHUNK = 64
DILATED_PATTERNS = ((128, 1), (512, 4), (2048, 16))
DIL_BLOCK = 128
SB_BLOCK = 128
D_FF = ((8 * D_MODEL // 3 + 127) // 128) * 128
RMS_EPS = 1e-6
NEG_BIG = -1e30

IN_SIZES = (
    GROUP_WIDTH, GROUP_WIDTH,
    GROUP_WIDTH, GROUP_WIDTH, GROUP_WIDTH, GROUP_WIDTH, N_HEADS, N_HEADS,
    GROUP_WIDTH, GROUP_WIDTH, GROUP_WIDTH,
    GROUP_WIDTH, GROUP_WIDTH, GROUP_WIDTH,
)
D_IN = sum(IN_SIZES)

kernel_name = 'hybrid_parallel_groups_rglru_mlstm_dilated_stickbreaking'


def _split_points():
    pts, acc = [], 0
    for s in IN_SIZES[:-1]:
        acc += s
        pts.append(acc)
    return pts


def alibi_slopes():
    return jnp.asarray(2.0 ** (-8.0 * np.arange(1, N_HEADS + 1) / N_HEADS), dtype=jnp.float32)


def rmsnorm(x, gain):
    xf = x.astype(jnp.float32)
    y = xf * lax.rsqrt(jnp.mean(xf * xf, axis=-1, keepdims=True) + RMS_EPS)
    return (y * gain.astype(jnp.float32)).astype(x.dtype)


def swiglu(x, w_gate, w_up, w_down):
    return (jax.nn.silu(x @ w_gate) * (x @ w_up)) @ w_down


def rglru_mixer(xr, gate, conv_w, conv_b, w_a, b_a, w_x, b_x, lam):
    bsz, seq, width = xr.shape
    xc = lax.conv_general_dilated(xr, conv_w[:, None, :], window_strides=(1,),
                                  padding=[(LRU_CONV_WIDTH - 1, 0)],
                                  dimension_numbers=('NWC', 'WIO', 'NWC'),
                                  feature_group_count=width) + conv_b
    xg = xc.reshape(bsz, seq, LRU_BLOCKS, LRU_BLOCK_WIDTH)
    r = jax.nn.sigmoid(jnp.einsum('bsnc,ncd->bsnd', xg, w_a).reshape(bsz, seq, width) + b_a)
    i = jax.nn.sigmoid(jnp.einsum('bsnc,ncd->bsnd', xg, w_x).reshape(bsz, seq, width) + b_x)
    log_a = -LRU_C * jax.nn.softplus(-lam.astype(jnp.float32)) * r.astype(jnp.float32)
    a = jnp.exp(log_a)
    u = jnp.sqrt(-jnp.expm1(2.0 * log_a)) * (i * xc).astype(jnp.float32)

    def combine(lhs, rhs):
        a1, b1 = lhs
        a2, b2 = rhs
        return a1 * a2, a2 * b1 + b2

    _, h = lax.associative_scan(combine, (a, u), axis=1)
    return h.astype(xr.dtype) * jax.nn.gelu(gate)


def mlstm_mixer(q, k, v, o_pre, i_pre, f_pre, ig_bias, fg_bias, head_gain):
    bsz, seq, _ = q.shape
    nh, dh, ln = N_HEADS, HEAD_DIM, MLSTM_CHUNK
    nc = seq // ln

    def to_chunks(t):
        return t.astype(jnp.float32).reshape(bsz, nc, ln, nh, dh).transpose(1, 0, 3, 2, 4)

    def gate_chunks(t):
        return t.astype(jnp.float32).reshape(bsz, nc, ln, nh).transpose(1, 0, 3, 2)

    qc = to_chunks(q)
    kc = to_chunks(k) * (dh ** -0.5)
    vc = to_chunks(v)
    igc = gate_chunks(i_pre + ig_bias)
    lfc = jax.nn.log_sigmoid(gate_chunks(f_pre + fg_bias))
    causal = jnp.tril(jnp.ones((ln, ln), dtype=bool))

    def step(carry, inp):
        c_mat, n_vec, m_run = carry
        qt, kt, vt, ig, lf = inp
        b = jnp.cumsum(lf, axis=-1)
        log_d = jnp.where(causal, b[..., :, None] - b[..., None, :] + ig[..., None, :], -jnp.inf)
        inter = b + m_run[..., None]
        m_t = jnp.maximum(inter, log_d.max(axis=-1))
        s = jnp.einsum('bhtd,bhsd->bhts', qt, kt) * jnp.exp(log_d - m_t[..., None])
        w_inter = jnp.exp(inter - m_t)
        num = jnp.einsum('bhts,bhsd->bhtd', s, vt) + w_inter[..., None] * jnp.einsum('bhtk,bhkv->bhtv', qt, c_mat)
        den = s.sum(axis=-1) + w_inter * jnp.einsum('bhtk,bhk->bht', qt, n_vec)
        h = num / jnp.maximum(jnp.abs(den), jnp.exp(-m_t))[..., None]
        b_last = b[..., -1]
        log_w = b_last[..., None] - b + ig
        m_next = jnp.maximum(b_last + m_run, log_w.max(axis=-1))
        w = jnp.exp(log_w - m_next[..., None])
        decay = jnp.exp(b_last + m_run - m_next)
        c_mat = decay[..., None, None] * c_mat + jnp.einsum('bhs,bhsk,bhsv->bhkv', w, kt, vt)
        n_vec = decay[..., None] * n_vec + jnp.einsum('bhs,bhsk->bhk', w, kt)
        return (c_mat, n_vec, m_next), h

    init = (jnp.zeros((bsz, nh, dh, dh), jnp.float32),
            jnp.zeros((bsz, nh, dh), jnp.float32),
            jnp.zeros((bsz, nh), jnp.float32))
    _, hs = lax.scan(step, init, (qc, kc, vc, igc, lfc))
    h = hs.transpose(1, 0, 3, 2, 4).reshape(bsz, seq, nh, dh)
    h = h * lax.rsqrt(jnp.mean(h * h, axis=-1, keepdims=True) + RMS_EPS) * head_gain.astype(jnp.float32).reshape(nh, dh)
    h = h.reshape(bsz, seq, nh * dh) * jax.nn.sigmoid(o_pre.astype(jnp.float32))
    return h.astype(q.dtype)


def dilated_branch(q, k, v, window, dilation, slopes):
    bsz, seq, nh, dh = q.shape
    span = window // dilation
    sub_len = seq // dilation
    nb = -(-sub_len // DIL_BLOCK)
    pad_len = nb * DIL_BLOCK

    def to_sub(t):
        t = t.reshape(bsz, sub_len, dilation, nh, dh).transpose(0, 2, 3, 1, 4)
        t = jnp.pad(t, ((0, 0), (0, 0), (0, 0), (0, pad_len - sub_len), (0, 0)))
        return t.reshape(bsz, dilation, nh, nb, DIL_BLOCK, dh)

    def with_prev(t):
        prev = jnp.pad(t, ((0, 0), (0, 0), (0, 0), (1, 0), (0, 0), (0, 0)))[:, :, :, :-1]
        return jnp.concatenate([prev, t], axis=4)

    qs = to_sub(q)
    ks = with_prev(to_sub(k))
    vs = with_prev(to_sub(v)).astype(jnp.float32)
    scores = jnp.einsum('brhnqd,brhnkd->brhnqk', qs, ks).astype(jnp.float32) * (dh ** -0.5)
    q_pos = jnp.arange(DIL_BLOCK)[:, None] + DIL_BLOCK
    k_pos = jnp.arange(2 * DIL_BLOCK)[None, :]
    dist = q_pos - k_pos
    k_abs = (jnp.arange(nb) * DIL_BLOCK - DIL_BLOCK)[:, None, None] + k_pos[None]
    valid = (dist >= 0) & (dist <= span) & (k_abs >= 0)
    bias = -slopes[:, None, None, None] * (dilation * dist).astype(jnp.float32)
    scores = jnp.where(valid, scores + bias, NEG_BIG)
    m = scores.max(axis=-1)
    p = jnp.exp(scores - m[..., None])
    den = p.sum(axis=-1)
    num = jnp.einsum('brhnqk,brhnkd->brhnqd', p, vs)

    def from_sub(t):
        t = t.reshape((bsz, dilation, nh, pad_len) + t.shape[5:])[:, :, :, :sub_len]
        t = jnp.moveaxis(t, 3, 1)
        return t.reshape((bsz, seq, nh) + t.shape[4:])

    return from_sub(num), from_sub(den), from_sub(m)


def dilated_mixer(q, k, v, q_gain, k_gain):
    bsz, seq, _ = q.shape
    shp = (bsz, seq, N_HEADS, HEAD_DIM)
    qh = rmsnorm(q.reshape(shp), q_gain)
    kh = rmsnorm(k.reshape(shp), k_gain)
    vh = v.reshape(shp)
    slopes = alibi_slopes()
    outs = [dilated_branch(qh, kh, vh, w, d, slopes) for (w, d) in DILATED_PATTERNS]
    nums = jnp.stack([o[0] for o in outs])
    dens = jnp.stack([o[1] for o in outs])
    ms = jnp.stack([o[2] for o in outs])
    wts = jnp.exp(ms - ms.max(axis=0, keepdims=True))
    out = (wts[..., None] * nums).sum(axis=0) / (wts * dens).sum(axis=0)[..., None]
    return out.reshape(bsz, seq, GROUP_WIDTH).astype(q.dtype)


def stick_breaking_mixer(q, k, v):
    bsz, seq, _ = q.shape

    def heads(t):
        return t.reshape(bsz, seq, N_HEADS, HEAD_DIM).transpose(0, 2, 1, 3)

    qh, kh = heads(q), heads(k)
    vh = heads(v).astype(jnp.float32)
    key_pos = jnp.arange(seq)

    def block(n):
        start = n * SB_BLOCK
        qb = lax.dynamic_slice_in_dim(qh, start, SB_BLOCK, axis=2)
        z = jnp.einsum('bhqd,bhkd->bhqk', qb, kh).astype(jnp.float32) * (HEAD_DIM ** -0.5)
        q_pos = start + jnp.arange(SB_BLOCK)
        mask = key_pos[None, :] < q_pos[:, None]
        log_keep = jnp.where(mask, jax.nn.log_sigmoid(-z), 0.0)
        log_keep_after = lax.cumsum(log_keep, axis=3, reverse=True) - log_keep
        weight = jnp.where(mask, jnp.exp(jax.nn.log_sigmoid(z) + log_keep_after), 0.0)
        return jnp.einsum('bhqk,bhkd->bhqd', weight, vh)

    out = lax.map(block, jnp.arange(seq // SB_BLOCK))
    return out.transpose(1, 0, 3, 2, 4).reshape(bsz, seq, GROUP_WIDTH).astype(q.dtype)


def setup_inputs(seed: int = 0) -> dict:
    key = jax.random.key(seed)
    ks = jax.random.split(key, 24)

    def nrm(k, shape, scale):
        return jax.random.normal(k, shape, jnp.float32) * scale

    def gain(k, shape):
        return 1.0 + 0.02 * jax.random.normal(k, shape, jnp.float32)

    nl = DEPTH
    u = jax.random.uniform(ks[13], (nl, GROUP_WIDTH), jnp.float32, 0.9, 0.999)
    a = u ** (1.0 / LRU_C)
    lru_lambda = jnp.log(a) - jnp.log1p(-a)
    fg_bias = jnp.linspace(3.0, 6.0, N_HEADS, dtype=jnp.float32)[None, :] + nrm(ks[15], (nl, N_HEADS), 0.02)
    return {
        'x': nrm(ks[0], (BATCH, SEQ, D_MODEL), 1.0),
        'ffn1_norm': gain(ks[1], (nl, D_MODEL)),
        'ffn1_w_gate': nrm(ks[2], (nl, D_MODEL, D_FF), D_MODEL ** -0.5),
        'ffn1_w_up': nrm(ks[3], (nl, D_MODEL, D_FF), D_MODEL ** -0.5),
        'ffn1_w_down': nrm(ks[4], (nl, D_FF, D_MODEL), D_FF ** -0.5),
        'mix_norm': gain(ks[5], (nl, D_MODEL)),
        'w_in': nrm(ks[6], (nl, D_MODEL, D_IN), D_MODEL ** -0.5),
        'lru_conv_w': nrm(ks[7], (nl, LRU_CONV_WIDTH, GROUP_WIDTH), LRU_CONV_WIDTH ** -0.5),
        'lru_conv_b': nrm(ks[8], (nl, GROUP_WIDTH), 0.02),
        'lru_w_a': nrm(ks[9], (nl, LRU_BLOCKS, LRU_BLOCK_WIDTH, LRU_BLOCK_WIDTH), LRU_BLOCK_WIDTH ** -0.5),
        'lru_b_a': nrm(ks[10], (nl, GROUP_WIDTH), 0.02),
        'lru_w_x': nrm(ks[11], (nl, LRU_BLOCKS, LRU_BLOCK_WIDTH, LRU_BLOCK_WIDTH), LRU_BLOCK_WIDTH ** -0.5),
        'lru_b_x': nrm(ks[12], (nl, GROUP_WIDTH), 0.02),
        'lru_lambda': lru_lambda,
        'mlstm_ig_bias': nrm(ks[14], (nl, N_HEADS), 0.1),
        'mlstm_fg_bias': fg_bias,
        'attn_q_gain': gain(ks[16], (nl, HEAD_DIM)),
        'attn_k_gain': gain(ks[17], (nl, HEAD_DIM)),
        'group_out_gain': gain(ks[18], (nl, MIX_WIDTH)),
        'w_out': nrm(ks[19], (nl, MIX_WIDTH, D_MODEL), MIX_WIDTH ** -0.5),
        'ffn2_norm': gain(ks[20], (nl, D_MODEL)),
        'ffn2_w_gate': nrm(ks[21], (nl, D_MODEL, D_FF), D_MODEL ** -0.5),
        'ffn2_w_up': nrm(ks[22], (nl, D_MODEL, D_FF), D_MODEL ** -0.5),
        'ffn2_w_down': nrm(ks[23], (nl, D_FF, D_MODEL), D_FF ** -0.5),
    }


def reference(x, ffn1_norm, ffn1_w_gate, ffn1_w_up, ffn1_w_down, mix_norm, w_in,
              lru_conv_w, lru_conv_b, lru_w_a, lru_b_a, lru_w_x, lru_b_x, lru_lambda,
              mlstm_ig_bias, mlstm_fg_bias, attn_q_gain, attn_k_gain, group_out_gain, w_out,
              ffn2_norm, ffn2_w_gate, ffn2_w_up, ffn2_w_down):
    split_pts = _split_points()
    for l in range(DEPTH):
        x = x + 0.5 * swiglu(rmsnorm(x, ffn1_norm[l]), ffn1_w_gate[l], ffn1_w_up[l], ffn1_w_down[l])
        h = rmsnorm(x, mix_norm[l])
        (lx, lg, mq, mk, mv, mo, mi, mf, cq, ck, cv, sq, sk, sv) = jnp.split(h @ w_in[l], split_pts, axis=-1)
        g_a, g_b, g_c, g_d = jnp.split(group_out_gain[l], N_MIXERS)
        y_a = rmsnorm(rglru_mixer(lx, lg, lru_conv_w[l], lru_conv_b[l], lru_w_a[l], lru_b_a[l],
                                  lru_w_x[l], lru_b_x[l], lru_lambda[l]), g_a)
        y_b = mlstm_mixer(mq, mk, mv, mo, mi, mf, mlstm_ig_bias[l], mlstm_fg_bias[l], g_b)
        y_c = rmsnorm(dilated_mixer(cq, ck, cv, attn_q_gain[l], attn_k_gain[l]), g_c)
        y_d = rmsnorm(stick_breaking_mixer(sq, sk, sv), g_d)
        mixed = jnp.concatenate([y_a, y_b, y_c, y_d], axis=-1).astype(x.dtype)
        x = x + mixed @ w_out[l]
        x = x + 0.5 * swiglu(rmsnorm(x, ffn2_norm[l]), ffn2_w_gate[l], ffn2_w_up[l], ffn2_w_down[l])
    return x
```

```python
import functools

import jax
import jax.numpy as jnp
from jax import lax
from jax.experimental import pallas as pl
from jax.experimental.pallas import tpu as pltpu

F32 = jnp.float32
BF16 = jnp.bfloat16

D_MODEL = 2048
N_HEADS = 4
HEAD_DIM = 128
GROUP_WIDTH = 512
N_PROJ_GROUPS = 12
D_MAIN = N_PROJ_GROUPS * GROUP_WIDTH
D_FF = 5504
FF_TILE = 512
D_FF_PAD = ((D_FF + FF_TILE - 1) // FF_TILE) * FF_TILE
LRU_C = 8.0
LRU_CONV_WIDTH = 4
DILATED_PATTERNS = ((128, 1), (512, 4), (2048, 16))
BLK = 128
RMS_EPS = 1e-6
NEG_BIG = -1e30
ATTN_SCALE = HEAD_DIM ** -0.5
GATE_LANES = 128
GATE_ROWS = 16
VMEM_LIMIT = 52 * 1024 * 1024

G_LX, G_LG, G_MQ, G_MK, G_MV, G_MO, G_CQ, G_CK, G_CV, G_SQ, G_SK, G_SV = range(12)

_NT = (((1,), (1,)), ((), ()))
_TN = (((0,), (0,)), ((), ()))


def _rms(x, gain):
    return x * lax.rsqrt(jnp.mean(x * x, axis=-1, keepdims=True) + RMS_EPS) * gain


def _softplus(x):
    return jnp.maximum(x, 0.0) + jnp.log1p(jnp.exp(-jnp.abs(x)))


def _log_sigmoid(x):
    return -_softplus(-x)


def _dot(a, b):
    return jnp.dot(a, b, preferred_element_type=F32)


def _dot_nt(a, b):
    return lax.dot_general(a, b, _NT, preferred_element_type=F32)


def _split_bf16(x, parts):
    out = []
    r = x
    for _ in range(parts):
        t = r.astype(BF16)
        out.append(t)
        r = r - t.astype(F32)
    return out


def _params(sem):
    return pltpu.CompilerParams(dimension_semantics=sem, vmem_limit_bytes=VMEM_LIMIT)


def _ffn_body(x_ref, g_ref, wg_ref, wu_ref, wd_ref, o_ref, h_ref, acc_ref):
    j = pl.program_id(1)

    @pl.when(j == 0)
    def _():
        h_ref[...] = _rms(x_ref[...], g_ref[...]).astype(BF16)
        acc_ref[...] = jnp.zeros_like(acc_ref)

    h = h_ref[...]
    g = _dot(h, wg_ref[...])
    u = _dot(h, wu_ref[...])
    a = (g * jax.nn.sigmoid(g)) * u
    acc_ref[...] += _dot(a.astype(BF16), wd_ref[...])

    @pl.when(j == pl.num_programs(1) - 1)
    def _():
        o_ref[...] = x_ref[...] + 0.5 * acc_ref[...]


def _ffn(x, gain, wg, wu, wd, tm=512):
    m, d = x.shape
    fp = wg.shape[1]
    return pl.pallas_call(
        _ffn_body,
        grid=(m // tm, fp // FF_TILE),
        in_specs=[
            pl.BlockSpec((tm, d), lambda i, j: (i, 0)),
            pl.BlockSpec((1, d), lambda i, j: (0, 0)),
            pl.BlockSpec((d, FF_TILE), lambda i, j: (0, j)),
            pl.BlockSpec((d, FF_TILE), lambda i, j: (0, j)),
            pl.BlockSpec((FF_TILE, d), lambda i, j: (j, 0)),
        ],
        out_specs=pl.BlockSpec((tm, d), lambda i, j: (i, 0)),
        out_shape=jax.ShapeDtypeStruct((m, d), F32),
        scratch_shapes=[pltpu.VMEM((tm, d), BF16), pltpu.VMEM((tm, d), F32)],
        compiler_params=_params(("parallel", "arbitrary")),
        name="ffn",
    )(x, gain, wg, wu, wd)


def _inproj_body(x_ref, g_ref, w_ref, wgc_ref, wgr_ref, p_ref, gc_ref, gr_ref, h_ref):
    j = pl.program_id(1)

    @pl.when(j == 0)
    def _():
        h = _rms(x_ref[...], g_ref[...]).astype(BF16)
        h_ref[...] = h
        gc_ref[...] = _dot(h, wgc_ref[...])
        gr_ref[...] = _dot_nt(wgr_ref[...], h)

    p_ref[...] = _dot(h_ref[...], w_ref[...])


def _inproj(x, gain, w_main, w_gate_c, w_gate_r, tm=512):
    m, d = x.shape
    return pl.pallas_call(
        _inproj_body,
        grid=(m // tm, N_PROJ_GROUPS),
        in_specs=[
            pl.BlockSpec((tm, d), lambda i, j: (i, 0)),
            pl.BlockSpec((1, d), lambda i, j: (0, 0)),
            pl.BlockSpec((d, GROUP_WIDTH), lambda i, j: (0, j)),
            pl.BlockSpec((d, GATE_LANES), lambda i, j: (0, 0)),
            pl.BlockSpec((GATE_ROWS, d), lambda i, j: (0, 0)),
        ],
        out_specs=[
            pl.BlockSpec((tm, GROUP_WIDTH), lambda i, j: (i, j)),
            pl.BlockSpec((tm, GATE_LANES), lambda i, j: (i, 0)),
            pl.BlockSpec((GATE_ROWS, tm), lambda i, j: (0, i)),
        ],
        out_shape=[
            jax.ShapeDtypeStruct((m, D_MAIN), F32),
            jax.ShapeDtypeStruct((m, GATE_LANES), F32),
            jax.ShapeDtypeStruct((GATE_ROWS, m), F32),
        ],
        scratch_shapes=[pltpu.VMEM((tm, d), BF16)],
        compiler_params=_params(("parallel", "arbitrary")),
        name="inproj",
    )(x, gain, w_main, w_gate_c, w_gate_r)


def _lru_body(xr_ref, gate_ref, cw_ref, cb_ref, wa_ref, ba_ref, wx_ref, bx_ref, lam_ref,
              gain_ref, o_ref, xbuf, hcar):
    t = pl.program_id(1)
    tt = xr_ref.shape[1]
    width = xr_ref.shape[2]
    pad = 8

    @pl.when(t == 0)
    def _():
        xbuf[0:pad, :] = jnp.zeros((pad, width), F32)
        hcar[...] = jnp.zeros_like(hcar)

    xr = xr_ref[0]
    xbuf[pad:pad + tt, :] = xr
    xc = cb_ref[...]
    for j in range(LRU_CONV_WIDTH):
        off = pad - (LRU_CONV_WIDTH - 1) + j
        xc = xc + cw_ref[j:j + 1, :] * xbuf[pl.ds(off, tt), :]
    xbuf[0:pad, :] = xr[tt - pad:tt, :]

    xcb = xc.astype(BF16)
    ra, rx = [], []
    for n in range(N_HEADS):
        blk = xcb[:, n * HEAD_DIM:(n + 1) * HEAD_DIM]
        ra.append(_dot(blk, wa_ref[n]))
        rx.append(_dot(blk, wx_ref[n]))
    r = jax.nn.sigmoid(jnp.concatenate(ra, axis=1) + ba_ref[...])
    i = jax.nn.sigmoid(jnp.concatenate(rx, axis=1) + bx_ref[...])
    log_a = (-LRU_C * _softplus(-lam_ref[...])) * r
    a = jnp.exp(log_a)
    u = jnp.sqrt(-jnp.tanh(log_a) * (a * a + 1.0)) * (i * xc)

    row = lax.broadcasted_iota(jnp.int32, (tt, width), 0)
    s = 1
    while s < tt:
        keep = row >= s
        a_sh = jnp.where(keep, pltpu.roll(a, s, 0), 1.0)
        u_sh = jnp.where(keep, pltpu.roll(u, s, 0), 0.0)
        u = a * u_sh + u
        a = a * a_sh
        s *= 2
    h = u + a * hcar[...]
    hcar[...] = h[tt - 1:tt, :]

    y = h * jax.nn.gelu(gate_ref[0])
    o_ref[0] = _rms(y, gain_ref[...]).astype(BF16)


def _lru(p3, conv_w, conv_b, w_a, b_a, w_x, b_x, lam, gain, tt=512):
    b, s, _ = p3.shape
    w = GROUP_WIDTH
    vec = pl.BlockSpec((1, w), lambda bi, ti: (0, 0))
    mat = pl.BlockSpec((N_HEADS, HEAD_DIM, HEAD_DIM), lambda bi, ti: (0, 0, 0))
    return pl.pallas_call(
        _lru_body,
        grid=(b, s // tt),
        in_specs=[
            pl.BlockSpec((1, tt, w), lambda bi, ti: (bi, ti, G_LX)),
            pl.BlockSpec((1, tt, w), lambda bi, ti: (bi, ti, G_LG)),
            pl.BlockSpec((LRU_CONV_WIDTH, w), lambda bi, ti: (0, 0)),
            vec, mat, vec, mat, vec, vec, vec,
        ],
        out_specs=pl.BlockSpec((1, tt, w), lambda bi, ti: (bi, ti, 0)),
        out_shape=jax.ShapeDtypeStruct((b, s, w), BF16),
        scratch_shapes=[pltpu.VMEM((tt + 8, w), F32), pltpu.VMEM((1, w), F32)],
        compiler_params=_params(("parallel", "arbitrary")),
        name="rglru",
    )(p3, p3, conv_w, conv_b, w_a, b_a, w_x, b_x, lam, gain)


def _mlstm_body(q_ref, k_ref, v_ref, og_ref, gc_ref, gr_ref, bc_ref, br_ref, hg_ref,
                out_ref, c_ref, m_ref):
    c = pl.program_id(1)
    ln = q_ref.shape[1]

    @pl.when(c == 0)
    def _():
        c_ref[...] = jnp.zeros_like(c_ref)
        m_ref[...] = jnp.zeros_like(m_ref)

    gcol = gc_ref[0] + bc_ref[...]
    grow = gr_ref[...] + br_ref[...]
    lf_col = _log_sigmoid(gcol)
    lf_row = _log_sigmoid(grow)
    ri = lax.broadcasted_iota(jnp.int32, (ln, ln), 0)
    ci = lax.broadcasted_iota(jnp.int32, (ln, ln), 1)
    causal = ri >= ci
    tri_l = jnp.where(causal, 1.0, 0.0).astype(BF16)
    tri_u = jnp.where(ri <= ci, 1.0, 0.0).astype(BF16)
    b_col = sum(_dot(tri_l, part) for part in _split_bf16(lf_col, 3))
    b_row = sum(_dot(part, tri_u) for part in _split_bf16(lf_row, 3))

    lane = lax.broadcasted_iota(jnp.int32, (ln, HEAD_DIM), 1)
    ones_blk = jnp.where(lane == 0, 1.0, 0.0).astype(BF16)

    for h in range(N_HEADS):
        sl = slice(h * HEAD_DIM, (h + 1) * HEAD_DIM)
        bc = b_col[:, N_HEADS + h:N_HEADS + h + 1]
        igc = gcol[:, h:h + 1]
        brow = b_row[N_HEADS + h:N_HEADS + h + 1, :]
        igr = grow[h:h + 1, :]
        m_run = m_ref[h:h + 1, 0:1]
        b_last = bc[ln - 1:ln, :]

        log_d = jnp.where(causal, bc - brow + igr, NEG_BIG)
        inter = bc + m_run
        m_t = jnp.maximum(inter, jnp.max(log_d, axis=1, keepdims=True))
        qh = q_ref[0, :, sl].astype(BF16)
        kf = k_ref[0, :, sl] * ATTN_SCALE
        vh = v_ref[0, :, sl].astype(BF16)
        v_aug = jnp.concatenate([vh, ones_blk], axis=1)
        smat = _dot_nt(qh, kf.astype(BF16)) * jnp.exp(log_d - m_t)
        w_inter = jnp.exp(inter - m_t)
        c_aug = c_ref[h]
        num_aug = _dot(smat.astype(BF16), v_aug) + w_inter * _dot(qh, c_aug.astype(BF16))
        num = num_aug[:, :HEAD_DIM]
        den = num_aug[:, HEAD_DIM:HEAD_DIM + 1]
        hh = num / jnp.maximum(jnp.abs(den), jnp.exp(-m_t))

        log_w = b_last - bc + igc
        m_next = jnp.maximum(b_last + m_run, jnp.max(log_w, axis=0, keepdims=True))
        w = jnp.exp(log_w - m_next)
        decay = jnp.exp(b_last + m_run - m_next)
        kw = (kf * w).astype(BF16)
        c_ref[h] = decay * c_aug + lax.dot_general(kw, v_aug, _TN, preferred_element_type=F32)
        m_ref[h:h + 1, :] = jnp.broadcast_to(m_next, (1, m_ref.shape[1]))

        hn = _rms(hh, hg_ref[:, sl])
        out_ref[0, :, sl] = (hn * jax.nn.sigmoid(og_ref[0, :, sl])).astype(BF16)


def _mlstm(p3, gates_c, gates_r, bias_c, bias_r, head_gain, ln=128):
    b, s, _ = p3.shape
    nc = s // ln
    w = GROUP_WIDTH

    def col(g):
        return pl.BlockSpec((1, ln, w), lambda bi, ci, g=g: (bi, ci, g))

    return pl.pallas_call(
        _mlstm_body,
        grid=(b, nc),
        in_specs=[
            col(G_MQ), col(G_MK), col(G_MV), col(G_MO),
            pl.BlockSpec((1, ln, GATE_LANES), lambda bi, ci: (bi, ci, 0)),
            pl.BlockSpec((GATE_ROWS, ln), lambda bi, ci: (0, bi * nc + ci)),
            pl.BlockSpec((1, GATE_LANES), lambda bi, ci: (0, 0)),
            pl.BlockSpec((GATE_ROWS, 1), lambda bi, ci: (0, 0)),
            pl.BlockSpec((1, w), lambda bi, ci: (0, 0)),
        ],
        out_specs=pl.BlockSpec((1, ln, w), lambda bi, ci: (bi, ci, 0)),
        out_shape=jax.ShapeDtypeStruct((b, s, w), BF16),
        scratch_shapes=[pltpu.VMEM((N_HEADS, HEAD_DIM, 2 * HEAD_DIM), F32),
                        pltpu.VMEM((8, 128), F32)],
        compiler_params=_params(("parallel", "arbitrary")),
        name="mlstm",
    )(p3, p3, p3, p3, gates_c.reshape(b, s, GATE_LANES), gates_r, bias_c, bias_r, head_gain)


def _dil_body(q_ref, k_ref, v_ref, qg_ref, kg_ref, sl_ref, o_ref, lse_ref, qs, ks, *, dilation):
    n_rows = q_ref.shape[1]
    nb = n_rows // BLK
    qs[...] = (_rms(q_ref[0], qg_ref[...]) * ATTN_SCALE).astype(BF16)
    ks[...] = _rms(k_ref[0], kg_ref[...]).astype(BF16)
    slope = sl_ref[0, 0:1, 0:1] * float(dilation)
    qq = lax.broadcasted_iota(jnp.int32, (BLK, BLK), 0)
    kk = lax.broadcasted_iota(jnp.int32, (BLK, BLK), 1)
    dist_c = qq - kk
    dist_p = dist_c + BLK
    bias_c = jnp.where(dist_c >= 0, -slope * dist_c.astype(F32), NEG_BIG)
    bias_p = jnp.where(dist_p <= BLK, -slope * dist_p.astype(F32), NEG_BIG)

    def block(n, carry):
        off = pl.multiple_of(n * BLK, BLK)
        poff = pl.multiple_of(jnp.maximum(n - 1, 0) * BLK, BLK)
        qn = qs[pl.ds(off, BLK), :]
        sc = _dot_nt(qn, ks[pl.ds(off, BLK), :]) + bias_c
        first = jnp.where(n > 0, 0.0, NEG_BIG)
        sp = jnp.maximum(_dot_nt(qn, ks[pl.ds(poff, BLK), :]) + bias_p + first, NEG_BIG)
        m = jnp.maximum(jnp.max(sc, axis=1, keepdims=True), jnp.max(sp, axis=1, keepdims=True))
        pc = jnp.exp(sc - m)
        pp = jnp.exp(sp - m)
        den = jnp.sum(pc, axis=1, keepdims=True) + jnp.sum(pp, axis=1, keepdims=True)
        num = (_dot(pc.astype(BF16), v_ref[0, pl.ds(off, BLK), :].astype(BF16))
               + _dot(pp.astype(BF16), v_ref[0, pl.ds(poff, BLK), :].astype(BF16)))
        o_ref[0, pl.ds(off, BLK), :] = num / den
        lse_ref[0, pl.ds(off, BLK), :] = jnp.broadcast_to(m + jnp.log(den), (BLK, HEAD_DIM))
        return carry

    lax.fori_loop(0, nb, block, 0)


def _dilated_pattern(p3, q_gain, k_gain, slopes, dilation):
    b, s, dm = p3.shape
    n_rows = s // dilation
    pd = p3.reshape(b, n_rows, dilation * dm)
    per_res = dm // HEAD_DIM

    def col(g):
        return pl.BlockSpec((1, n_rows, HEAD_DIM),
                            lambda bi, ri, hi, g=g: (bi, 0, ri * per_res + g * N_HEADS + hi))

    out_spec = pl.BlockSpec((1, n_rows, HEAD_DIM), lambda bi, ri, hi: (bi, 0, ri * N_HEADS + hi))
    vec = pl.BlockSpec((1, HEAD_DIM), lambda bi, ri, hi: (0, 0))
    o, lse = pl.pallas_call(
        functools.partial(_dil_body, dilation=dilation),
        grid=(b, dilation, N_HEADS),
        in_specs=[col(G_CQ), col(G_CK), col(G_CV), vec, vec,
                  pl.BlockSpec((1, 8, HEAD_DIM), lambda bi, ri, hi: (hi, 0, 0))],
        out_specs=[out_spec, out_spec],
        out_shape=[jax.ShapeDtypeStruct((b, n_rows, dilation * GROUP_WIDTH), F32)] * 2,
        scratch_shapes=[pltpu.VMEM((n_rows, HEAD_DIM), BF16), pltpu.VMEM((n_rows, HEAD_DIM), BF16)],
        compiler_params=_params(("parallel", "parallel", "parallel")),
        name=f"dilated_d{dilation}",
    )(pd, pd, pd, q_gain, k_gain, slopes)
    return o.reshape(b, s, GROUP_WIDTH), lse.reshape(b, s, GROUP_WIDTH)


def _merge_body(o1, o2, o3, l1, l2, l3, gain_ref, out_ref):
    la, lb, lc = l1[0], l2[0], l3[0]
    top = jnp.maximum(jnp.maximum(la, lb), lc)
    wa, wb, wc = jnp.exp(la - top), jnp.exp(lb - top), jnp.exp(lc - top)
    y = (wa * o1[0] + wb * o2[0] + wc * o3[0]) / (wa + wb + wc)
    out_ref[0] = _rms(y, gain_ref[...]).astype(BF16)


def _dilated_merge(outs, gain, tt=512):
    (o1, l1), (o2, l2), (o3, l3) = outs
    b, s, w = o1.shape
    blk = pl.BlockSpec((1, tt, w), lambda bi, ti: (bi, ti, 0))
    return pl.pallas_call(
        _merge_body,
        grid=(b, s // tt),
        in_specs=[blk] * 6 + [pl.BlockSpec((1, w), lambda bi, ti: (0, 0))],
        out_specs=blk,
        out_shape=jax.ShapeDtypeStruct((b, s, w), BF16),
        compiler_params=_params(("parallel", "parallel")),
        name="dilated_merge",
    )(o1, o2, o3, l1, l2, l3, gain)


def _sb_body(q_ref, k_ref, v_ref, o_ref, ks, vs):
    i = pl.program_id(2)

    @pl.when(i == 0)
    def _():
        ks[...] = k_ref[0].astype(BF16)
        vs[...] = v_ref[0].astype(BF16)

    q = q_ref[0].astype(BF16)
    qq = lax.broadcasted_iota(jnp.int32, (BLK, BLK), 0)
    kk = lax.broadcasted_iota(jnp.int32, (BLK, BLK), 1)
    strict = kk < qq
    after = jnp.concatenate([jnp.where(qq > kk, 1.0, 0.0), jnp.ones((BLK, BLK), F32)], axis=1).astype(BF16)

    def step(j, c, acc, masked):
        off = pl.multiple_of(j * BLK, BLK)
        z = _dot_nt(q, ks[pl.ds(off, BLK), :]) * ATTN_SCALE
        sp = _softplus(z)
        log_keep = -sp
        log_beta = z - sp
        if masked:
            log_keep = jnp.where(strict, log_keep, 0.0)
        sums = sum(_dot(part, after) for part in _split_bf16(log_keep, 2))
        w = jnp.exp(log_beta + sums[:, :BLK] + c)
        if masked:
            w = jnp.where(strict, w, 0.0)
        acc = acc + _dot(w.astype(BF16), vs[pl.ds(off, BLK), :])
        return c + sums[:, BLK:], acc

    zero = jnp.zeros((BLK, HEAD_DIM), F32)
    c0, acc0 = step(i, zero, zero, True)

    def body(jj, carry):
        return step(i - 1 - jj, carry[0], carry[1], False)

    _, acc = lax.fori_loop(0, i, body, (c0, acc0))
    o_ref[0] = acc


def _stick_breaking(p3):
    b, s, dm = p3.shape
    return pl.pallas_call(
        _sb_body,
        grid=(b, N_HEADS, s // BLK),
        in_specs=[
            pl.BlockSpec((1, BLK, HEAD_DIM), lambda bi, hi, qi: (bi, qi, G_SQ * N_HEADS + hi)),
            pl.BlockSpec((1, s, HEAD_DIM), lambda bi, hi, qi: (bi, 0, G_SK * N_HEADS + hi)),
            pl.BlockSpec((1, s, HEAD_DIM), lambda bi, hi, qi: (bi, 0, G_SV * N_HEADS + hi)),
        ],
        out_specs=pl.BlockSpec((1, BLK, HEAD_DIM), lambda bi, hi, qi: (bi, qi, hi)),
        out_shape=jax.ShapeDtypeStruct((b, s, GROUP_WIDTH), F32),
        scratch_shapes=[pltpu.VMEM((s, HEAD_DIM), BF16), pltpu.VMEM((s, HEAD_DIM), BF16)],
        compiler_params=_params(("parallel", "parallel", "arbitrary")),
        name="stick_breaking",
    )(p3, p3, p3)


def _outproj_body(x_ref, ya_ref, yb_ref, yc_ref, yd_ref, gd_ref, w_ref, o_ref):
    yd = _rms(yd_ref[...], gd_ref[...]).astype(BF16)
    w = GROUP_WIDTH
    acc = _dot(ya_ref[...], w_ref[0:w, :])
    acc += _dot(yb_ref[...], w_ref[w:2 * w, :])
    acc += _dot(yc_ref[...], w_ref[2 * w:3 * w, :])
    acc += _dot(yd, w_ref[3 * w:4 * w, :])
    o_ref[...] = x_ref[...] + acc


def _outproj(x, ya, yb, yc, yd, gain_d, w_out, tm=512):
    m, d = x.shape
    w = GROUP_WIDTH
    yblk = pl.BlockSpec((tm, w), lambda i: (i, 0))
    return pl.pallas_call(
        _outproj_body,
        grid=(m // tm,),
        in_specs=[pl.BlockSpec((tm, d), lambda i: (i, 0)), yblk, yblk, yblk, yblk,
                  pl.BlockSpec((1, w), lambda i: (0, 0)),
                  pl.BlockSpec((4 * w, d), lambda i: (0, 0))],
        out_specs=pl.BlockSpec((tm, d), lambda i: (i, 0)),
        out_shape=jax.ShapeDtypeStruct((m, d), F32),
        compiler_params=_params(("parallel",)),
        name="outproj",
    )(x, ya, yb, yc, yd, gain_d, w_out)


def _ffn_weights(w_gate, w_up, w_down):
    padc = ((0, 0), (0, D_FF_PAD - D_FF))
    padr = ((0, D_FF_PAD - D_FF), (0, 0))
    return (jnp.pad(w_gate.astype(BF16), padc), jnp.pad(w_up.astype(BF16), padc),
            jnp.pad(w_down.astype(BF16), padr))


def _row(v):
    return v.reshape(1, -1).astype(F32)


def kernel(x, ffn1_norm, ffn1_w_gate, ffn1_w_up, ffn1_w_down, mix_norm, w_in, lru_conv_w, lru_conv_b, lru_w_a, lru_b_a, lru_w_x, lru_b_x, lru_lambda, mlstm_ig_bias, mlstm_fg_bias, attn_q_gain, attn_k_gain, group_out_gain, w_out, ffn2_norm, ffn2_w_gate, ffn2_w_up, ffn2_w_down):
    b, s, d = x.shape
    depth = w_in.shape[0]
    m = b * s
    w = GROUP_WIDTH
    gate_lo = 6 * w
    gate_hi = gate_lo + 2 * N_HEADS
    slopes = 2.0 ** (-8.0 * jnp.arange(1, N_HEADS + 1, dtype=F32) / N_HEADS)
    slopes = jnp.broadcast_to(slopes[:, None, None], (N_HEADS, 8, HEAD_DIM))

    xf = x.reshape(m, d)
    for l in range(depth):
        xf = _ffn(xf, _row(ffn1_norm[l]), *_ffn_weights(ffn1_w_gate[l], ffn1_w_up[l], ffn1_w_down[l]))

        wl = w_in[l]
        w_main = jnp.concatenate([wl[:, :gate_lo], wl[:, gate_hi:]], axis=1).astype(BF16)
        w_gate = wl[:, gate_lo:gate_hi].astype(BF16)
        w_gate_c = jnp.pad(w_gate, ((0, 0), (0, GATE_LANES - 2 * N_HEADS)))
        w_gate_r = jnp.pad(w_gate.T, ((0, GATE_ROWS - 2 * N_HEADS), (0, 0)))
        p, gates_c, gates_r = _inproj(xf, _row(mix_norm[l]), w_main, w_gate_c, w_gate_r)
        p3 = p.reshape(b, s, D_MAIN)

        gains = group_out_gain[l].reshape(4, 1, w)
        ya = _lru(p3, lru_conv_w[l], _row(lru_conv_b[l]), lru_w_a[l].astype(BF16), _row(lru_b_a[l]),
                  lru_w_x[l].astype(BF16), _row(lru_b_x[l]), _row(lru_lambda[l]), gains[0])

        gate_bias = jnp.concatenate([mlstm_ig_bias[l], mlstm_fg_bias[l]]).astype(F32)
        bias_c = jnp.pad(gate_bias, (0, GATE_LANES - 2 * N_HEADS)).reshape(1, GATE_LANES)
        bias_r = jnp.pad(gate_bias, (0, GATE_ROWS - 2 * N_HEADS)).reshape(GATE_ROWS, 1)
        yb = _mlstm(p3, gates_c, gates_r, bias_c, bias_r, gains[1])

        outs = [_dilated_pattern(p3, _row(attn_q_gain[l]), _row(attn_k_gain[l]), slopes, dil)
                for (_, dil) in DILATED_PATTERNS]
        yc = _dilated_merge(outs, gains[2])

        yd = _stick_breaking(p3)

        xf = _outproj(xf, ya.reshape(m, w), yb.reshape(m, w), yc.reshape(m, w), yd.reshape(m, w),
                      gains[3], w_out[l].astype(BF16))

        xf = _ffn(xf, _row(ffn2_norm[l]), *_ffn_weights(ffn2_w_gate[l], ffn2_w_up[l], ffn2_w_down[l]))
    return xf.reshape(b, s, d)
```

```python
import jax
import jax.numpy as jnp
from jax import lax
from jax.experimental import pallas as pl
from jax.experimental.pallas import tpu as pltpu

F32 = jnp.float32
BF16 = jnp.bfloat16

D_MODEL = 2048
N_HEADS = 4
HEAD_DIM = 128
GROUP_WIDTH = 512
D_FF = 5504
FF_TILE = 512
D_FF_PAD = ((D_FF + FF_TILE - 1) // FF_TILE) * FF_TILE
LRU_C = 8.0
LRU_CONV_WIDTH = 4
DILATED_PATTERNS = ((128, 1), (512, 4), (2048, 16))
BLK = 128
SB_BLK = 256
RMS_EPS = 1e-6
NEG_BIG = -1e30
ATTN_SCALE = HEAD_DIM ** -0.5
GATE_LANES = 128
GATE_ROWS = 16
VMEM_LIMIT = 52 * 1024 * 1024

PF_LX, PF_LG, PF_MO, PF_CQ, PF_CK, PF_CV = range(6)
PB_MQ, PB_MK, PB_MV, PB_SQ, PB_SK, PB_SV = range(6)
D_HALF = 6 * GROUP_WIDTH
PROJ_TILE = 1024

_NT = (((1,), (1,)), ((), ()))
_TN = (((0,), (0,)), ((), ()))


def _rms(x, gain):
    return x * lax.rsqrt(jnp.mean(x * x, axis=-1, keepdims=True) + RMS_EPS) * gain


def _softplus(x):
    return jnp.maximum(x, 0.0) + jnp.log(1.0 + jnp.exp(-jnp.abs(x)))


def _log_sigmoid(x):
    return -_softplus(-x)


def _dot(a, b):
    return jnp.dot(a, b, preferred_element_type=F32)


def _dot_nt(a, b):
    return lax.dot_general(a, b, _NT, preferred_element_type=F32)


def _split_bf16(x, parts):
    out = []
    r = x
    for _ in range(parts):
        t = r.astype(BF16)
        out.append(t)
        r = r - t.astype(F32)
    return out


def _params(sem):
    return pltpu.CompilerParams(dimension_semantics=sem, vmem_limit_bytes=VMEM_LIMIT)


def _ffn_body(x_ref, g_ref, wg_ref, wu_ref, wd_ref, o_ref, h_ref, acc_ref):
    j = pl.program_id(1)

    @pl.when(j == 0)
    def _():
        h_ref[...] = _rms(x_ref[...], g_ref[...]).astype(BF16)
        acc_ref[...] = jnp.zeros_like(acc_ref)

    h = h_ref[...]
    g = _dot(h, wg_ref[...])
    u = _dot(h, wu_ref[...])
    a = (g * jax.nn.sigmoid(g)) * u
    acc_ref[...] += _dot(a.astype(BF16), wd_ref[...])

    @pl.when(j == pl.num_programs(1) - 1)
    def _():
        o_ref[...] = x_ref[...] + 0.5 * acc_ref[...]


def _ffn(x, gain, weights, layer, tm=512):
    wg, wu, wd = weights
    m, d = x.shape
    fp = wg.shape[2]
    return pl.pallas_call(
        _ffn_body,
        grid=(m // tm, fp // FF_TILE),
        in_specs=[
            pl.BlockSpec((tm, d), lambda i, j: (i, 0)),
            pl.BlockSpec((1, d), lambda i, j: (0, 0)),
            pl.BlockSpec((None, d, FF_TILE), lambda i, j: (layer, 0, j)),
            pl.BlockSpec((None, d, FF_TILE), lambda i, j: (layer, 0, j)),
            pl.BlockSpec((None, FF_TILE, d), lambda i, j: (layer, j, 0)),
        ],
        out_specs=pl.BlockSpec((tm, d), lambda i, j: (i, 0)),
        out_shape=jax.ShapeDtypeStruct((m, d), F32),
        scratch_shapes=[pltpu.VMEM((tm, d), BF16), pltpu.VMEM((tm, d), F32)],
        compiler_params=_params(("parallel", "arbitrary")),
        name="ffn",
    )(x, gain, wg, wu, wd)


def _inproj_body(x_ref, g_ref, wf_ref, wb_ref, sc_ref, wgc_ref, wgr_ref,
                 pf_ref, pb_ref, gc_ref, gr_ref, h_ref):
    j = pl.program_id(1)

    @pl.when(j == 0)
    def _():
        h = _rms(x_ref[...], g_ref[...]).astype(BF16)
        h_ref[...] = h
        gc_ref[...] = _dot(h, wgc_ref[...])
        gr_ref[...] = _dot_nt(wgr_ref[...], h)

    h = h_ref[...]
    pf_ref[...] = _dot(h, wf_ref[...])
    pb_ref[...] = (_dot(h, wb_ref[...]) * sc_ref[...]).astype(BF16)


def _inproj(x, gain, w_f, w_b, col_scale, w_gate_c, w_gate_r, tm=512):
    m, d = x.shape
    tn = PROJ_TILE
    return pl.pallas_call(
        _inproj_body,
        grid=(m // tm, D_HALF // tn),
        in_specs=[
            pl.BlockSpec((tm, d), lambda i, j: (i, 0)),
            pl.BlockSpec((1, d), lambda i, j: (0, 0)),
            pl.BlockSpec((d, tn), lambda i, j: (0, j)),
            pl.BlockSpec((d, tn), lambda i, j: (0, j)),
            pl.BlockSpec((1, tn), lambda i, j: (0, j)),
            pl.BlockSpec((d, GATE_LANES), lambda i, j: (0, 0)),
            pl.BlockSpec((GATE_ROWS, d), lambda i, j: (0, 0)),
        ],
        out_specs=[
            pl.BlockSpec((tm, tn), lambda i, j: (i, j)),
            pl.BlockSpec((tm, tn), lambda i, j: (i, j)),
            pl.BlockSpec((tm, GATE_LANES), lambda i, j: (i, 0)),
            pl.BlockSpec((GATE_ROWS, tm), lambda i, j: (0, i)),
        ],
        out_shape=[
            jax.ShapeDtypeStruct((m, D_HALF), F32),
            jax.ShapeDtypeStruct((m, D_HALF), BF16),
            jax.ShapeDtypeStruct((m, GATE_LANES), F32),
            jax.ShapeDtypeStruct((GATE_ROWS, m), F32),
        ],
        scratch_shapes=[pltpu.VMEM((tm, d), BF16)],
        compiler_params=_params(("parallel", "arbitrary")),
        name="inproj",
    )(x, gain, w_f, w_b, col_scale, w_gate_c, w_gate_r)


def _lru_body(xr_ref, gate_ref, cw_ref, cb_ref, wa_ref, ba_ref, wx_ref, bx_ref, lam_ref,
              gain_ref, o_ref, xbuf, hcar):
    t = pl.program_id(1)
    tt = xr_ref.shape[1]
    width = xr_ref.shape[2]
    pad = 8

    @pl.when(t == 0)
    def _():
        xbuf[0:pad, :] = jnp.zeros((pad, width), F32)
        hcar[...] = jnp.zeros_like(hcar)

    xr = xr_ref[0]
    xbuf[pad:pad + tt, :] = xr
    xc = cb_ref[...]
    for j in range(LRU_CONV_WIDTH):
        off = pad - (LRU_CONV_WIDTH - 1) + j
        xc = xc + cw_ref[j:j + 1, :] * xbuf[pl.ds(off, tt), :]
    xbuf[0:pad, :] = xr[tt - pad:tt, :]

    xcb = xc.astype(BF16)
    ra, rx = [], []
    for n in range(N_HEADS):
        blk = xcb[:, n * HEAD_DIM:(n + 1) * HEAD_DIM]
        ra.append(_dot(blk, wa_ref[n]))
        rx.append(_dot(blk, wx_ref[n]))
    r = jax.nn.sigmoid(jnp.concatenate(ra, axis=1) + ba_ref[...])
    i = jax.nn.sigmoid(jnp.concatenate(rx, axis=1) + bx_ref[...])
    log_a = (-LRU_C * _softplus(-lam_ref[...])) * r
    a = jnp.exp(log_a)
    u = jnp.sqrt(-jnp.tanh(log_a) * (a * a + 1.0)) * (i * xc)

    row = lax.broadcasted_iota(jnp.int32, (tt, width), 0)
    s = 1
    while s < tt:
        keep = row >= s
        a_sh = jnp.where(keep, pltpu.roll(a, s, 0), 1.0)
        u_sh = jnp.where(keep, pltpu.roll(u, s, 0), 0.0)
        u = a * u_sh + u
        a = a * a_sh
        s *= 2
    h = u + a * hcar[...]
    hcar[...] = h[tt - 1:tt, :]

    y = h * jax.nn.gelu(gate_ref[0])
    o_ref[0] = _rms(y, gain_ref[...]).astype(BF16)


def _lru(pf3, conv_w, conv_b, w_a, b_a, w_x, b_x, lam, gain, tt=512):
    b, s, _ = pf3.shape
    w = GROUP_WIDTH
    vec = pl.BlockSpec((1, w), lambda bi, ti: (0, 0))
    mat = pl.BlockSpec((N_HEADS, HEAD_DIM, HEAD_DIM), lambda bi, ti: (0, 0, 0))
    return pl.pallas_call(
        _lru_body,
        grid=(b, s // tt),
        in_specs=[
            pl.BlockSpec((1, tt, w), lambda bi, ti: (bi, ti, PF_LX)),
            pl.BlockSpec((1, tt, w), lambda bi, ti: (bi, ti, PF_LG)),
            pl.BlockSpec((LRU_CONV_WIDTH, w), lambda bi, ti: (0, 0)),
            vec, mat, vec, mat, vec, vec, vec,
        ],
        out_specs=pl.BlockSpec((1, tt, w), lambda bi, ti: (bi, ti, 0)),
        out_shape=jax.ShapeDtypeStruct((b, s, w), BF16),
        scratch_shapes=[pltpu.VMEM((tt + 8, w), F32), pltpu.VMEM((1, w), F32)],
        compiler_params=_params(("parallel", "arbitrary")),
        name="rglru",
    )(pf3, pf3, conv_w, conv_b, w_a, b_a, w_x, b_x, lam, gain)


def _mlstm_body(q_ref, k_ref, v_ref, og_ref, gc_ref, gr_ref, bc_ref, br_ref, hg_ref,
                out_ref, c_ref, m_ref):
    c = pl.program_id(1)
    ln = q_ref.shape[1]

    @pl.when(c == 0)
    def _():
        c_ref[...] = jnp.zeros_like(c_ref)
        m_ref[...] = jnp.zeros_like(m_ref)

    gcol = gc_ref[0] + bc_ref[...]
    grow = gr_ref[...] + br_ref[...]
    lf_col = _log_sigmoid(gcol)
    lf_row = _log_sigmoid(grow)
    ri = lax.broadcasted_iota(jnp.int32, (ln, ln), 0)
    ci = lax.broadcasted_iota(jnp.int32, (ln, ln), 1)
    causal = ri >= ci
    tri_l = jnp.where(causal, 1.0, 0.0).astype(BF16)
    tri_u = jnp.where(ri <= ci, 1.0, 0.0).astype(BF16)
    b_col = sum(_dot(tri_l, part) for part in _split_bf16(lf_col, 3))
    b_row = sum(_dot(part, tri_u) for part in _split_bf16(lf_row, 3))

    lane = lax.broadcasted_iota(jnp.int32, (ln, HEAD_DIM), 1)
    ones_blk = jnp.where(lane == 0, 1.0, 0.0).astype(BF16)

    for h in range(N_HEADS):
        sl = slice(h * HEAD_DIM, (h + 1) * HEAD_DIM)
        bc = b_col[:, N_HEADS + h:N_HEADS + h + 1]
        igc = gcol[:, h:h + 1]
        brow = b_row[N_HEADS + h:N_HEADS + h + 1, :]
        igr = grow[h:h + 1, :]
        m_run = m_ref[h:h + 1, 0:1]
        b_last = bc[ln - 1:ln, :]

        log_d = jnp.where(causal, bc - brow + igr, NEG_BIG)
        inter = bc + m_run
        m_t = jnp.maximum(inter, jnp.max(log_d, axis=1, keepdims=True))
        qh = q_ref[0, :, sl]
        kh = k_ref[0, :, sl]
        vh = v_ref[0, :, sl]
        v_aug = jnp.concatenate([vh, ones_blk], axis=1)
        smat = _dot_nt(qh, kh) * jnp.exp(log_d - m_t)
        w_inter = jnp.exp(inter - m_t)
        c_aug = c_ref[h]
        num_aug = _dot(smat.astype(BF16), v_aug) + w_inter * _dot(qh, c_aug.astype(BF16))
        num = num_aug[:, :HEAD_DIM]
        den = num_aug[:, HEAD_DIM:HEAD_DIM + 1]
        hh = num / jnp.maximum(jnp.abs(den), jnp.exp(-m_t))

        log_w = b_last - bc + igc
        m_next = jnp.maximum(b_last + m_run, jnp.max(log_w, axis=0, keepdims=True))
        w = jnp.exp(log_w - m_next)
        decay = jnp.exp(b_last + m_run - m_next)
        kw = (kh.astype(F32) * w).astype(BF16)
        c_ref[h] = decay * c_aug + lax.dot_general(kw, v_aug, _TN, preferred_element_type=F32)
        m_ref[h:h + 1, :] = jnp.broadcast_to(m_next, (1, m_ref.shape[1]))

        hn = _rms(hh, hg_ref[:, sl])
        out_ref[0, :, sl] = (hn * jax.nn.sigmoid(og_ref[0, :, sl])).astype(BF16)


def _mlstm(pf3, pb3, gates_c, gates_r, bias_c, bias_r, head_gain, ln=128):
    b, s, _ = pf3.shape
    nc = s // ln
    w = GROUP_WIDTH

    def col(g):
        return pl.BlockSpec((1, ln, w), lambda bi, ci, g=g: (bi, ci, g))

    return pl.pallas_call(
        _mlstm_body,
        grid=(b, nc),
        in_specs=[
            col(PB_MQ), col(PB_MK), col(PB_MV), col(PF_MO),
            pl.BlockSpec((1, ln, GATE_LANES), lambda bi, ci: (bi, ci, 0)),
            pl.BlockSpec((GATE_ROWS, ln), lambda bi, ci: (0, bi * nc + ci)),
            pl.BlockSpec((1, GATE_LANES), lambda bi, ci: (0, 0)),
            pl.BlockSpec((GATE_ROWS, 1), lambda bi, ci: (0, 0)),
            pl.BlockSpec((1, w), lambda bi, ci: (0, 0)),
        ],
        out_specs=pl.BlockSpec((1, ln, w), lambda bi, ci: (bi, ci, 0)),
        out_shape=jax.ShapeDtypeStruct((b, s, w), BF16),
        scratch_shapes=[pltpu.VMEM((N_HEADS, HEAD_DIM, 2 * HEAD_DIM), F32),
                        pltpu.VMEM((8, 128), F32)],
        compiler_params=_params(("parallel", "arbitrary")),
        name="mlstm",
    )(pb3, pb3, pb3, pf3, gates_c.reshape(b, s, GATE_LANES), gates_r, bias_c, bias_r, head_gain)


def _dil_body(q_ref, k_ref, v_ref, qg_ref, kg_ref, sl_ref, o_ref, qn, kn, m_s, l_s):
    s = q_ref.shape[1]
    n_blocks = s // BLK
    vv, o2 = v_ref.at[0], o_ref.at[0]
    qn[...] = _rms(q_ref[0], qg_ref[...]) * ATTN_SCALE
    kn[...] = _rms(k_ref[0], kg_ref[...])
    slope = sl_ref[0, 0:1, 0:1]
    qq = lax.broadcasted_iota(jnp.int32, (BLK, 2 * BLK), 0)
    kk = lax.broadcasted_iota(jnp.int32, (BLK, 2 * BLK), 1)
    dist = jnp.where(kk < BLK, qq - kk, qq - kk + 2 * BLK)
    in_window = jnp.logical_and(dist >= 0, dist <= BLK)
    prev_lanes = lax.broadcasted_iota(jnp.int32, (1, 2 * BLK), 1) >= BLK
    ones = jnp.ones((2 * BLK, HEAD_DIM), BF16)
    half = n_blocks // 2

    for pi, (window, dil) in enumerate(DILATED_PATTERNS):
        assert window // dil == BLK and s % (dil * BLK) == 0
        nb = s // (dil * BLK)
        bias = jnp.where(in_window, (-float(dil) * slope) * dist.astype(F32), NEG_BIG)
        first, last = pi == 0, pi == len(DILATED_PATTERNS) - 1

        def rows(start, dil=dil):
            return pl.ds(start, BLK) if dil == 1 else pl.ds(start, BLK, stride=dil)

        def compute(t, dil=dil, nb=nb, bias=bias, first=first, rows=rows):
            r = t // nb
            n = t - r * nb
            cur = rows(r + dil * BLK * n)
            prev = rows(r + dil * BLK * jnp.maximum(n - 1, 0))
            qb = qn[cur, :].astype(BF16)
            k2 = jnp.concatenate([kn[cur, :], kn[prev, :]], axis=0).astype(BF16)
            v2 = jnp.concatenate([vv[cur, :], vv[prev, :]], axis=0).astype(BF16)
            no_prev = jnp.where(prev_lanes, jnp.where(n > 0, 0.0, NEG_BIG), 0.0)
            sc = jnp.maximum(_dot_nt(qb, k2) + bias + no_prev, NEG_BIG)
            m_b = jnp.max(sc, axis=1, keepdims=True)
            p = jnp.exp(sc - m_b).astype(BF16)
            nd = _dot(p, jnp.concatenate([v2, ones], axis=1))
            num, den = nd[:, :HEAD_DIM], nd[:, HEAD_DIM:]
            if first:
                return cur, jnp.broadcast_to(m_b, (BLK, HEAD_DIM)), den, num
            m_o = m_s[cur, :]
            m_n = jnp.maximum(m_o, m_b)
            a_o = jnp.exp(m_o - m_n)
            a_b = jnp.exp(m_b - m_n)
            return cur, m_n, l_s[cur, :] * a_o + den * a_b, o2[cur, :] * a_o + num * a_b

        def store(res, last=last):
            cur, m_n, l_n, acc = res
            if last:
                o2[cur, :] = acc / l_n
            else:
                m_s[cur, :] = m_n
                l_s[cur, :] = l_n
                o2[cur, :] = acc

        def pair(t, carry, compute=compute, store=store):
            ra, rb = compute(t), compute(t + half)
            store(ra)
            store(rb)
            return carry

        lax.fori_loop(0, half, pair, 0)


def _dilated(pf3, q_gain, k_gain, slopes):
    b, s, _ = pf3.shape

    def col(g):
        return pl.BlockSpec((1, s, HEAD_DIM), lambda bi, hi, g=g: (bi, 0, g * N_HEADS + hi))

    vec = pl.BlockSpec((1, HEAD_DIM), lambda bi, hi: (0, 0))
    return pl.pallas_call(
        _dil_body,
        grid=(b, N_HEADS),
        in_specs=[col(PF_CQ), col(PF_CK), col(PF_CV), vec, vec,
                  pl.BlockSpec((1, 8, HEAD_DIM), lambda bi, hi: (hi, 0, 0))],
        out_specs=pl.BlockSpec((1, s, HEAD_DIM), lambda bi, hi: (bi, 0, hi)),
        out_shape=jax.ShapeDtypeStruct((b, s, GROUP_WIDTH), F32),
        scratch_shapes=[pltpu.VMEM((s, HEAD_DIM), F32)] * 4,
        compiler_params=_params(("parallel", "parallel")),
        name="dilated",
    )(pf3, pf3, pf3, q_gain, k_gain, slopes)


def _sb_body(q_ref, k_ref, v_ref, o_ref):
    i = pl.program_id(1)
    nq = SB_BLK
    qq = lax.broadcasted_iota(jnp.int32, (nq, nq), 0)
    kk = lax.broadcasted_iota(jnp.int32, (nq, nq), 1)
    strict = kk < qq
    after = jnp.where(qq > kk, 1.0, 0.0).astype(BF16)
    o_ref[...] = jnp.zeros_like(o_ref)

    def step(j, gone, masked):
        keys = pl.ds(pl.multiple_of(j * nq, nq), nq)
        heads = [slice(h * HEAD_DIM, (h + 1) * HEAD_DIM) for h in range(N_HEADS)]
        zs = [_dot_nt(q_ref[0, :, sl], k_ref[0, keys, sl]) for sl in heads]
        sps = [_softplus(z) for z in zs]
        drops = [jnp.where(strict, sp, 0.0) for sp in sps] if masked else sps
        parts = [_split_bf16(drop, 2) for drop in drops]
        laters = [_dot(hi, after) + _dot(lo, after) for hi, lo in parts]
        ws = [jnp.exp(z - sp - later - g) for z, sp, later, g in zip(zs, sps, laters, gone)]
        if masked:
            ws = [jnp.where(strict, w, 0.0) for w in ws]
        pv = [_dot(w.astype(BF16), v_ref[0, keys, sl]) for w, sl in zip(ws, heads)]
        o_ref[0] += jnp.concatenate(pv, axis=1)
        return tuple(g + jnp.sum(drop, axis=1, keepdims=True) for g, drop in zip(gone, drops))

    gone = step(i, tuple(jnp.zeros((nq, 1), F32) for _ in range(N_HEADS)), True)
    lax.fori_loop(0, i, lambda jj, g: step(i - 1 - jj, g, False), gone)


def _stick_breaking(pb3):
    b, s, _ = pb3.shape
    w = GROUP_WIDTH
    return pl.pallas_call(
        _sb_body,
        grid=(b, s // SB_BLK),
        in_specs=[
            pl.BlockSpec((1, SB_BLK, w), lambda bi, qi: (bi, qi, PB_SQ)),
            pl.BlockSpec((1, s, w), lambda bi, qi: (bi, 0, PB_SK)),
            pl.BlockSpec((1, s, w), lambda bi, qi: (bi, 0, PB_SV)),
        ],
        out_specs=pl.BlockSpec((1, SB_BLK, w), lambda bi, qi: (bi, qi, 0)),
        out_shape=jax.ShapeDtypeStruct((b, s, w), F32),
        compiler_params=_params(("parallel", "parallel")),
        name="stick_breaking",
    )(pb3, pb3, pb3)


def _outproj_body(x_ref, ya_ref, yb_ref, yc_ref, yd_ref, gc_ref, gd_ref, w_ref, o_ref):
    yc = _rms(yc_ref[...], gc_ref[...]).astype(BF16)
    yd = _rms(yd_ref[...], gd_ref[...]).astype(BF16)
    w = GROUP_WIDTH
    acc = _dot(ya_ref[...], w_ref[0:w, :])
    acc += _dot(yb_ref[...], w_ref[w:2 * w, :])
    acc += _dot(yc, w_ref[2 * w:3 * w, :])
    acc += _dot(yd, w_ref[3 * w:4 * w, :])
    o_ref[...] = x_ref[...] + acc


def _outproj(x, ya, yb, yc, yd, gain_c, gain_d, w_out, tm=512):
    m, d = x.shape
    w = GROUP_WIDTH
    yblk = pl.BlockSpec((tm, w), lambda i: (i, 0))
    vec = pl.BlockSpec((1, w), lambda i: (0, 0))
    return pl.pallas_call(
        _outproj_body,
        grid=(m // tm,),
        in_specs=[pl.BlockSpec((tm, d), lambda i: (i, 0)), yblk, yblk, yblk, yblk, vec, vec,
                  pl.BlockSpec((4 * w, d), lambda i: (0, 0))],
        out_specs=pl.BlockSpec((tm, d), lambda i: (i, 0)),
        out_shape=jax.ShapeDtypeStruct((m, d), F32),
        compiler_params=_params(("parallel",)),
        name="outproj",
    )(x, ya, yb, yc, yd, gain_c, gain_d, w_out)


def _ffn_weights(w_gate, w_up, w_down):
    padc = ((0, 0), (0, 0), (0, D_FF_PAD - D_FF))
    padr = ((0, 0), (0, D_FF_PAD - D_FF), (0, 0))
    return (jnp.pad(w_gate.astype(BF16), padc), jnp.pad(w_up.astype(BF16), padc),
            jnp.pad(w_down.astype(BF16), padr))


def _row(v):
    return v.reshape(1, -1).astype(F32)


def _split_w_in(wl):
    w = GROUP_WIDTH
    gate_lo = 6 * w
    gate_hi = gate_lo + 2 * N_HEADS
    w_f = jnp.concatenate([wl[:, 0:2 * w], wl[:, 5 * w:6 * w], wl[:, gate_hi:gate_hi + 3 * w]], axis=1)
    w_b = jnp.concatenate([wl[:, 2 * w:5 * w], wl[:, gate_hi + 3 * w:]], axis=1)
    return w_f.astype(BF16), w_b.astype(BF16), wl[:, gate_lo:gate_hi].astype(BF16)


def kernel(x, ffn1_norm, ffn1_w_gate, ffn1_w_up, ffn1_w_down, mix_norm, w_in, lru_conv_w, lru_conv_b, lru_w_a, lru_b_a, lru_w_x, lru_b_x, lru_lambda, mlstm_ig_bias, mlstm_fg_bias, attn_q_gain, attn_k_gain, group_out_gain, w_out, ffn2_norm, ffn2_w_gate, ffn2_w_up, ffn2_w_down):
    b, s, d = x.shape
    depth = w_in.shape[0]
    m = b * s
    w = GROUP_WIDTH
    slopes = 2.0 ** (-8.0 * jnp.arange(1, N_HEADS + 1, dtype=F32) / N_HEADS)
    slopes = jnp.broadcast_to(slopes[:, None, None], (N_HEADS, 8, HEAD_DIM))
    col_scale = jnp.ones((6, w), F32).at[PB_MK].set(ATTN_SCALE).at[PB_SQ].set(ATTN_SCALE).reshape(1, D_HALF)

    ffn1_w = _ffn_weights(ffn1_w_gate, ffn1_w_up, ffn1_w_down)
    ffn2_w = _ffn_weights(ffn2_w_gate, ffn2_w_up, ffn2_w_down)

    xf = x.reshape(m, d)
    for l in range(depth):
        xf = _ffn(xf, _row(ffn1_norm[l]), ffn1_w, l)

        w_f, w_b, w_gate = _split_w_in(w_in[l])
        w_gate_c = jnp.pad(w_gate, ((0, 0), (0, GATE_LANES - 2 * N_HEADS)))
        w_gate_r = jnp.pad(w_gate.T, ((0, GATE_ROWS - 2 * N_HEADS), (0, 0)))
        pf, pb, gates_c, gates_r = _inproj(xf, _row(mix_norm[l]), w_f, w_b, col_scale, w_gate_c, w_gate_r)
        pf3 = pf.reshape(b, s, D_HALF)
        pb3 = pb.reshape(b, s, D_HALF)

        gains = group_out_gain[l].reshape(4, 1, w)
        ya = _lru(pf3, lru_conv_w[l], _row(lru_conv_b[l]), lru_w_a[l].astype(BF16), _row(lru_b_a[l]),
                  lru_w_x[l].astype(BF16), _row(lru_b_x[l]), _row(lru_lambda[l]), gains[0])

        gate_bias = jnp.concatenate([mlstm_ig_bias[l], mlstm_fg_bias[l]]).astype(F32)
        bias_c = jnp.pad(gate_bias, (0, GATE_LANES - 2 * N_HEADS)).reshape(1, GATE_LANES)
        bias_r = jnp.pad(gate_bias, (0, GATE_ROWS - 2 * N_HEADS)).reshape(GATE_ROWS, 1)
        yb = _mlstm(pf3, pb3, gates_c, gates_r, bias_c, bias_r, gains[1])

        yc = _dilated(pf3, _row(attn_q_gain[l]), _row(attn_k_gain[l]), slopes)
        yd = _stick_breaking(pb3)

        xf = _outproj(xf, ya.reshape(m, w), yb.reshape(m, w), yc.reshape(m, w), yd.reshape(m, w),
                      gains[2], gains[3], w_out[l].astype(BF16))

        xf = _ffn(xf, _row(ffn2_norm[l]), ffn2_w, l)
    return xf.reshape(b, s, d)
```

```python
import jax
import jax.numpy as jnp
from jax import lax
from jax.experimental import pallas as pl
from jax.experimental.pallas import tpu as pltpu

F32 = jnp.float32
BF16 = jnp.bfloat16

D_MODEL = 2048
N_HEADS = 4
HEAD_DIM = 128
GROUP_WIDTH = 512
D_FF = 5504
FF_TILE = 512
D_FF_PAD = ((D_FF + FF_TILE - 1) // FF_TILE) * FF_TILE
LRU_C = 8.0
LRU_CONV_WIDTH = 4
DILATED_PATTERNS = ((128, 1), (512, 4), (2048, 16))
BLK = 128
SB_BLK = 256
DIL_PAR = 4
RMS_EPS = 1e-6
NEG_BIG = -1e30
ATTN_SCALE = HEAD_DIM ** -0.5
SB_Q_SCALE = ATTN_SCALE * 1.4426950408889634
GATE_LANES = 128
GATE_ROWS = 16
VMEM_LIMIT = 52 * 1024 * 1024

PF_LX, PF_LG, PF_MO, PF_CQ, PF_CK, PF_CV = range(6)
PB_MQ, PB_MK, PB_MV, PB_SQ, PB_SK, PB_SV = range(6)
D_HALF = 6 * GROUP_WIDTH
PROJ_TILE = 1024

_NT = (((1,), (1,)), ((), ()))
_TN = (((0,), (0,)), ((), ()))


def _rms(x, gain):
    return x * lax.rsqrt(jnp.mean(x * x, axis=-1, keepdims=True) + RMS_EPS) * gain


def _softplus(x):
    return jnp.maximum(x, 0.0) + jnp.log(1.0 + jnp.exp(-jnp.abs(x)))


def _log_sigmoid(x):
    return -_softplus(-x)


def _dot(a, b):
    return jnp.dot(a, b, preferred_element_type=F32)


def _dot_nt(a, b):
    return lax.dot_general(a, b, _NT, preferred_element_type=F32)


def _split_bf16(x, parts):
    out = []
    r = x
    for _ in range(parts):
        t = r.astype(BF16)
        out.append(t)
        r = r - t.astype(F32)
    return out


def _params(sem):
    return pltpu.CompilerParams(dimension_semantics=sem, vmem_limit_bytes=VMEM_LIMIT)


def _ffn_body(x_ref, g_ref, wg_ref, wu_ref, wd_ref, o_ref, h_ref, acc_ref):
    j = pl.program_id(1)

    @pl.when(j == 0)
    def _():
        h_ref[...] = _rms(x_ref[...], g_ref[...]).astype(BF16)
        acc_ref[...] = jnp.zeros_like(acc_ref)

    h = h_ref[...]
    g = _dot(h, wg_ref[...])
    u = _dot(h, wu_ref[...])
    a = (g * jax.nn.sigmoid(g)) * u
    acc_ref[...] += _dot(a.astype(BF16), wd_ref[...])

    @pl.when(j == pl.num_programs(1) - 1)
    def _():
        o_ref[...] = x_ref[...] + 0.5 * acc_ref[...]


def _ffn(x, gain, weights, layer, tm=512):
    wg, wu, wd = weights
    m, d = x.shape
    fp = wg.shape[2]
    return pl.pallas_call(
        _ffn_body,
        grid=(m // tm, fp // FF_TILE),
        in_specs=[
            pl.BlockSpec((tm, d), lambda i, j: (i, 0)),
            pl.BlockSpec((1, d), lambda i, j: (0, 0)),
            pl.BlockSpec((None, d, FF_TILE), lambda i, j: (layer, 0, j)),
            pl.BlockSpec((None, d, FF_TILE), lambda i, j: (layer, 0, j)),
            pl.BlockSpec((None, FF_TILE, d), lambda i, j: (layer, j, 0)),
        ],
        out_specs=pl.BlockSpec((tm, d), lambda i, j: (i, 0)),
        out_shape=jax.ShapeDtypeStruct((m, d), F32),
        scratch_shapes=[pltpu.VMEM((tm, d), BF16), pltpu.VMEM((tm, d), F32)],
        compiler_params=_params(("parallel", "arbitrary")),
        name="ffn",
    )(x, gain, wg, wu, wd)


def _inproj_body(x_ref, g_ref, wf_ref, wb_ref, sc_ref, wgc_ref, wgr_ref,
                 pf_ref, pb_ref, gc_ref, gr_ref, h_ref):
    j = pl.program_id(1)

    @pl.when(j == 0)
    def _():
        h = _rms(x_ref[...], g_ref[...]).astype(BF16)
        h_ref[...] = h
        gc_ref[...] = _dot(h, wgc_ref[...])
        gr_ref[...] = _dot_nt(wgr_ref[...], h)

    h = h_ref[...]
    pf_ref[...] = _dot(h, wf_ref[...])
    pb_ref[...] = (_dot(h, wb_ref[...]) * sc_ref[...]).astype(BF16)


def _inproj(x, gain, w_f, w_b, col_scale, w_gate_c, w_gate_r, tm=512):
    m, d = x.shape
    tn = PROJ_TILE
    return pl.pallas_call(
        _inproj_body,
        grid=(m // tm, D_HALF // tn),
        in_specs=[
            pl.BlockSpec((tm, d), lambda i, j: (i, 0)),
            pl.BlockSpec((1, d), lambda i, j: (0, 0)),
            pl.BlockSpec((d, tn), lambda i, j: (0, j)),
            pl.BlockSpec((d, tn), lambda i, j: (0, j)),
            pl.BlockSpec((1, tn), lambda i, j: (0, j)),
            pl.BlockSpec((d, GATE_LANES), lambda i, j: (0, 0)),
            pl.BlockSpec((GATE_ROWS, d), lambda i, j: (0, 0)),
        ],
        out_specs=[
            pl.BlockSpec((tm, tn), lambda i, j: (i, j)),
            pl.BlockSpec((tm, tn), lambda i, j: (i, j)),
            pl.BlockSpec((tm, GATE_LANES), lambda i, j: (i, 0)),
            pl.BlockSpec((GATE_ROWS, tm), lambda i, j: (0, i)),
        ],
        out_shape=[
            jax.ShapeDtypeStruct((m, D_HALF), F32),
            jax.ShapeDtypeStruct((m, D_HALF), BF16),
            jax.ShapeDtypeStruct((m, GATE_LANES), F32),
            jax.ShapeDtypeStruct((GATE_ROWS, m), F32),
        ],
        scratch_shapes=[pltpu.VMEM((tm, d), BF16)],
        compiler_params=_params(("parallel", "arbitrary")),
        name="inproj",
    )(x, gain, w_f, w_b, col_scale, w_gate_c, w_gate_r)


def _lru_body(xr_ref, gate_ref, cw_ref, cb_ref, wa_ref, ba_ref, wx_ref, bx_ref, lam_ref,
              gain_ref, o_ref, xbuf, hcar):
    t = pl.program_id(1)
    tt = xr_ref.shape[1]
    width = xr_ref.shape[2]
    pad = 8

    @pl.when(t == 0)
    def _():
        xbuf[0:pad, :] = jnp.zeros((pad, width), F32)
        hcar[...] = jnp.zeros_like(hcar)

    xr = xr_ref[0]
    xbuf[pad:pad + tt, :] = xr
    xc = cb_ref[...]
    for j in range(LRU_CONV_WIDTH):
        off = pad - (LRU_CONV_WIDTH - 1) + j
        xc = xc + cw_ref[j:j + 1, :] * xbuf[pl.ds(off, tt), :]
    xbuf[0:pad, :] = xr[tt - pad:tt, :]

    xcb = xc.astype(BF16)
    ra, rx = [], []
    for n in range(N_HEADS):
        blk = xcb[:, n * HEAD_DIM:(n + 1) * HEAD_DIM]
        ra.append(_dot(blk, wa_ref[n]))
        rx.append(_dot(blk, wx_ref[n]))
    r = jax.nn.sigmoid(jnp.concatenate(ra, axis=1) + ba_ref[...])
    i = jax.nn.sigmoid(jnp.concatenate(rx, axis=1) + bx_ref[...])
    log_a = (-LRU_C * _softplus(-lam_ref[...])) * r
    a = jnp.exp(log_a)
    u = jnp.sqrt(-jnp.tanh(log_a) * (a * a + 1.0)) * (i * xc)

    row = lax.broadcasted_iota(jnp.int32, (tt, width), 0)
    s = 1
    while s < tt:
        keep = row >= s
        a_sh = jnp.where(keep, pltpu.roll(a, s, 0), 1.0)
        u_sh = jnp.where(keep, pltpu.roll(u, s, 0), 0.0)
        u = a * u_sh + u
        a = a * a_sh
        s *= 2
    h = u + a * hcar[...]
    hcar[...] = h[tt - 1:tt, :]

    y = h * jax.nn.gelu(gate_ref[0])
    o_ref[0] = _rms(y, gain_ref[...]).astype(BF16)


def _lru(pf3, conv_w, conv_b, w_a, b_a, w_x, b_x, lam, gain, tt=512):
    b, s, _ = pf3.shape
    w = GROUP_WIDTH
    vec = pl.BlockSpec((1, w), lambda bi, ti: (0, 0))
    mat = pl.BlockSpec((N_HEADS, HEAD_DIM, HEAD_DIM), lambda bi, ti: (0, 0, 0))
    return pl.pallas_call(
        _lru_body,
        grid=(b, s // tt),
        in_specs=[
            pl.BlockSpec((1, tt, w), lambda bi, ti: (bi, ti, PF_LX)),
            pl.BlockSpec((1, tt, w), lambda bi, ti: (bi, ti, PF_LG)),
            pl.BlockSpec((LRU_CONV_WIDTH, w), lambda bi, ti: (0, 0)),
            vec, mat, vec, mat, vec, vec, vec,
        ],
        out_specs=pl.BlockSpec((1, tt, w), lambda bi, ti: (bi, ti, 0)),
        out_shape=jax.ShapeDtypeStruct((b, s, w), BF16),
        scratch_shapes=[pltpu.VMEM((tt + 8, w), F32), pltpu.VMEM((1, w), F32)],
        compiler_params=_params(("parallel", "arbitrary")),
        name="rglru",
    )(pf3, pf3, conv_w, conv_b, w_a, b_a, w_x, b_x, lam, gain)


def _mlstm_body(q_ref, k_ref, v_ref, og_ref, gc_ref, gr_ref, bc_ref, br_ref, hg_ref,
                out_ref, c_ref, m_ref):
    c = pl.program_id(1)
    ln = q_ref.shape[1]

    @pl.when(c == 0)
    def _():
        c_ref[...] = jnp.zeros_like(c_ref)
        m_ref[...] = jnp.zeros_like(m_ref)

    gcol = gc_ref[0] + bc_ref[...]
    grow = gr_ref[...] + br_ref[...]
    lf_col = _log_sigmoid(gcol)
    lf_row = _log_sigmoid(grow)
    ri = lax.broadcasted_iota(jnp.int32, (ln, ln), 0)
    ci = lax.broadcasted_iota(jnp.int32, (ln, ln), 1)
    causal = ri >= ci
    tri_l = jnp.where(causal, 1.0, 0.0).astype(BF16)
    tri_u = jnp.where(ri <= ci, 1.0, 0.0).astype(BF16)
    b_col = sum(_dot(tri_l, part) for part in _split_bf16(lf_col, 3))
    b_row = sum(_dot(part, tri_u) for part in _split_bf16(lf_row, 3))

    ones_blk = jnp.ones((ln, HEAD_DIM), BF16)
    reps = ln // HEAD_DIM

    def wide(x):
        return jnp.concatenate([x] * reps, axis=1)

    heads = [slice(h * HEAD_DIM, (h + 1) * HEAD_DIM) for h in range(N_HEADS)]
    qs = [q_ref[0, :, sl] for sl in heads]
    ks = [k_ref[0, :, sl] for sl in heads]
    v_augs = [jnp.concatenate([v_ref[0, :, sl], ones_blk], axis=1) for sl in heads]
    c_augs = [c_ref[h] for h in range(N_HEADS)]

    decay_w, m_ts, w_inters, m_nexts, decays, kws = [], [], [], [], [], []
    for h in range(N_HEADS):
        bc = jnp.broadcast_to(b_col[:, N_HEADS + h:N_HEADS + h + 1], (ln, HEAD_DIM))
        ig = jnp.broadcast_to(gcol[:, h:h + 1], (ln, HEAD_DIM))
        brow = b_row[N_HEADS + h:N_HEADS + h + 1, :]
        igr = grow[h:h + 1, :]
        m_run = m_ref[h:h + 1, :]
        b_last = bc[ln - 1:ln, :]
        log_d = jnp.where(causal, wide(bc) - brow + igr, NEG_BIG)
        inter = bc + m_run
        m_t = jnp.maximum(inter, jnp.max(log_d, axis=1, keepdims=True))
        decay_w.append(jnp.exp(log_d - wide(m_t)))
        m_ts.append(m_t)
        w_inters.append(jnp.exp(inter - m_t))
        log_w = b_last - bc + ig
        m_next = jnp.maximum(b_last + m_run, jnp.max(log_w, axis=0, keepdims=True))
        m_nexts.append(m_next)
        decays.append(jnp.exp(b_last + m_run - m_next))
        kws.append((ks[h].astype(F32) * jnp.exp(log_w - m_next)).astype(BF16))

    qk = [_dot_nt(q, k) for q, k in zip(qs, ks)]
    qc = [_dot(q, c_aug.astype(BF16)) for q, c_aug in zip(qs, c_augs)]
    smats = [(s * d).astype(BF16) for s, d in zip(qk, decay_w)]
    intra = [_dot(s, v_aug) for s, v_aug in zip(smats, v_augs)]
    upd = [lax.dot_general(kw, v_aug, _TN, preferred_element_type=F32) for kw, v_aug in zip(kws, v_augs)]

    for h, sl in enumerate(heads):
        num = intra[h][:, :HEAD_DIM] + w_inters[h] * qc[h][:, :HEAD_DIM]
        den = intra[h][:, HEAD_DIM:] + w_inters[h] * qc[h][:, HEAD_DIM:]
        hh = num / jnp.maximum(jnp.abs(den), jnp.exp(-m_ts[h]))
        c_ref[h] = jnp.concatenate([decays[h]] * 2, axis=1) * c_augs[h] + upd[h]
        m_ref[h:h + 1, :] = m_nexts[h]
        hn = _rms(hh, hg_ref[:, sl])
        out_ref[0, :, sl] = (hn * jax.nn.sigmoid(og_ref[0, :, sl])).astype(BF16)


def _mlstm(pf3, pb3, gates_c, gates_r, bias_c, bias_r, head_gain, ln=256):
    b, s, _ = pf3.shape
    nc = s // ln
    w = GROUP_WIDTH

    def col(g):
        return pl.BlockSpec((1, ln, w), lambda bi, ci, g=g: (bi, ci, g))

    return pl.pallas_call(
        _mlstm_body,
        grid=(b, nc),
        in_specs=[
            col(PB_MQ), col(PB_MK), col(PB_MV), col(PF_MO),
            pl.BlockSpec((1, ln, GATE_LANES), lambda bi, ci: (bi, ci, 0)),
            pl.BlockSpec((GATE_ROWS, ln), lambda bi, ci: (0, bi * nc + ci)),
            pl.BlockSpec((1, GATE_LANES), lambda bi, ci: (0, 0)),
            pl.BlockSpec((GATE_ROWS, 1), lambda bi, ci: (0, 0)),
            pl.BlockSpec((1, w), lambda bi, ci: (0, 0)),
        ],
        out_specs=pl.BlockSpec((1, ln, w), lambda bi, ci: (bi, ci, 0)),
        out_shape=jax.ShapeDtypeStruct((b, s, w), BF16),
        scratch_shapes=[pltpu.VMEM((N_HEADS, HEAD_DIM, 2 * HEAD_DIM), F32),
                        pltpu.VMEM((8, 128), F32)],
        compiler_params=_params(("parallel", "arbitrary")),
        name="mlstm",
    )(pb3, pb3, pb3, pf3, gates_c.reshape(b, s, GATE_LANES), gates_r, bias_c, bias_r, head_gain)


def _dil_body(q_ref, k_ref, v_ref, qg_ref, kg_ref, sl_ref, o_ref, qn, kn, m_s, l_s):
    s = q_ref.shape[1]
    n_blocks = s // BLK
    vv, o2 = v_ref.at[0], o_ref.at[0]
    qn[...] = _rms(q_ref[0], qg_ref[...]) * ATTN_SCALE
    kn[...] = _rms(k_ref[0], kg_ref[...])
    slope = sl_ref[0, 0:1, 0:1]
    qq = lax.broadcasted_iota(jnp.int32, (BLK, 2 * BLK), 0)
    kk = lax.broadcasted_iota(jnp.int32, (BLK, 2 * BLK), 1)
    dist = jnp.where(kk < BLK, qq - kk, qq - kk + 2 * BLK)
    in_window = jnp.logical_and(dist >= 0, dist <= BLK)
    prev_lanes = lax.broadcasted_iota(jnp.int32, (1, 2 * BLK), 1) >= BLK
    ones = jnp.ones((2 * BLK, HEAD_DIM), BF16)
    assert n_blocks % DIL_PAR == 0

    for pi, (window, dil) in enumerate(DILATED_PATTERNS):
        assert window // dil == BLK and s % (dil * BLK) == 0
        nb = s // (dil * BLK)
        bias = jnp.where(in_window, (-float(dil) * slope) * dist.astype(F32), NEG_BIG)
        first, last = pi == 0, pi == len(DILATED_PATTERNS) - 1

        def rows(start, dil=dil):
            return pl.ds(start, BLK) if dil == 1 else pl.ds(start, BLK, stride=dil)

        def group(t0, carry, dil=dil, nb=nb, bias=bias, first=first, last=last, rows=rows):
            cur, qb, k2, v2, no_prev = [], [], [], [], []
            for i in range(DIL_PAR):
                t = t0 + i * (n_blocks // DIL_PAR)
                r = t // nb
                n = t - r * nb
                c = rows(r + dil * BLK * n)
                p = rows(r + dil * BLK * jnp.maximum(n - 1, 0))
                cur.append(c)
                qb.append(qn[c, :].astype(BF16))
                k2.append(jnp.concatenate([kn[c, :], kn[p, :]], axis=0).astype(BF16))
                v2.append(jnp.concatenate([vv[c, :], vv[p, :]], axis=0).astype(BF16))
                no_prev.append(jnp.where(prev_lanes, jnp.where(n > 0, 0.0, NEG_BIG), 0.0))
            sc = [jnp.maximum(_dot_nt(q, k) + bias + off, NEG_BIG) for q, k, off in zip(qb, k2, no_prev)]
            m_b = [jnp.max(x, axis=1, keepdims=True) for x in sc]
            pr = [jnp.exp(x - m).astype(BF16) for x, m in zip(sc, m_b)]
            nd = [_dot(p, jnp.concatenate([v, ones], axis=1)) for p, v in zip(pr, v2)]
            res = []
            for c, m, x in zip(cur, m_b, nd):
                num, den = x[:, :HEAD_DIM], x[:, HEAD_DIM:]
                if first:
                    res.append((jnp.broadcast_to(m, (BLK, HEAD_DIM)), den, num))
                else:
                    m_o = m_s[c, :]
                    m_n = jnp.maximum(m_o, m)
                    a_o = jnp.exp(m_o - m_n)
                    a_b = jnp.exp(m - m_n)
                    res.append((m_n, l_s[c, :] * a_o + den * a_b, o2[c, :] * a_o + num * a_b))
            for c, (m_n, l_n, acc) in zip(cur, res):
                if last:
                    o2[c, :] = acc / l_n
                else:
                    m_s[c, :] = m_n
                    l_s[c, :] = l_n
                    o2[c, :] = acc
            return carry

        lax.fori_loop(0, n_blocks // DIL_PAR, group, 0)


def _dilated(pf3, q_gain, k_gain, slopes):
    b, s, _ = pf3.shape

    def col(g):
        return pl.BlockSpec((1, s, HEAD_DIM), lambda bi, hi, g=g: (bi, 0, g * N_HEADS + hi))

    vec = pl.BlockSpec((1, HEAD_DIM), lambda bi, hi: (0, 0))
    return pl.pallas_call(
        _dil_body,
        grid=(b, N_HEADS),
        in_specs=[col(PF_CQ), col(PF_CK), col(PF_CV), vec, vec,
                  pl.BlockSpec((1, 8, HEAD_DIM), lambda bi, hi: (hi, 0, 0))],
        out_specs=pl.BlockSpec((1, s, HEAD_DIM), lambda bi, hi: (bi, 0, hi)),
        out_shape=jax.ShapeDtypeStruct((b, s, GROUP_WIDTH), F32),
        scratch_shapes=[pltpu.VMEM((s, HEAD_DIM), F32)] * 4,
        compiler_params=_params(("parallel", "parallel")),
        name="dilated",
    )(pf3, pf3, pf3, q_gain, k_gain, slopes)


def _sb_body(q_ref, k_ref, v_ref, o_ref, z_ref):
    i = pl.program_id(1)
    nq = SB_BLK
    qq = lax.broadcasted_iota(jnp.int32, (nq, nq), 0)
    kk = lax.broadcasted_iota(jnp.int32, (nq, nq), 1)
    strict = kk < qq
    after = jnp.where(qq > kk, 1.0, 0.0).astype(BF16)
    after2 = jnp.concatenate([after, after], axis=0)
    o_ref[...] = jnp.zeros_like(o_ref)

    heads = [slice(h * HEAD_DIM, (h + 1) * HEAD_DIM) for h in range(N_HEADS)]

    def logits(j):
        keys = pl.ds(pl.multiple_of(j * nq, nq), nq)
        return [_dot_nt(q_ref[0, :, sl], k_ref[0, keys, sl]) for sl in heads]

    def step(j, gone, masked):
        keys = pl.ds(pl.multiple_of(j * nq, nq), nq)
        zs = [z_ref[h] for h in range(N_HEADS)]
        sps = [jnp.maximum(z, 0.0) + jnp.log2(1.0 + jnp.exp2(-jnp.abs(z))) for z in zs]
        drops = [jnp.where(strict, sp, 0.0) for sp in sps] if masked else sps
        for h, z in enumerate(logits(jnp.maximum(j - 1, 0))):
            z_ref[h] = z
        parts = [jnp.concatenate(_split_bf16(drop, 2), axis=1) for drop in drops]
        laters = [_dot(part, after2) for part in parts]
        ws = [jnp.exp2(z - sp - later - g) for z, sp, later, g in zip(zs, sps, laters, gone)]
        if masked:
            ws = [jnp.where(strict, w, 0.0) for w in ws]
        pv = [_dot(w.astype(BF16), v_ref[0, keys, sl]) for w, sl in zip(ws, heads)]
        o_ref[0] += jnp.concatenate(pv, axis=1)
        return tuple(g + jnp.sum(drop, axis=1, keepdims=True) for g, drop in zip(gone, drops))

    for h, z in enumerate(logits(i)):
        z_ref[h] = z
    gone = step(i, tuple(jnp.zeros((nq, 1), F32) for _ in range(N_HEADS)), True)
    lax.fori_loop(0, i, lambda jj, g: step(i - 1 - jj, g, False), gone)


def _stick_breaking(pb3):
    b, s, _ = pb3.shape
    w = GROUP_WIDTH
    return pl.pallas_call(
        _sb_body,
        grid=(b, s // SB_BLK),
        in_specs=[
            pl.BlockSpec((1, SB_BLK, w), lambda bi, qi: (bi, qi, PB_SQ)),
            pl.BlockSpec((1, s, w), lambda bi, qi: (bi, 0, PB_SK)),
            pl.BlockSpec((1, s, w), lambda bi, qi: (bi, 0, PB_SV)),
        ],
        out_specs=pl.BlockSpec((1, SB_BLK, w), lambda bi, qi: (bi, qi, 0)),
        out_shape=jax.ShapeDtypeStruct((b, s, w), F32),
        scratch_shapes=[pltpu.VMEM((N_HEADS, SB_BLK, SB_BLK), F32)],
        compiler_params=_params(("parallel", "parallel")),
        name="stick_breaking",
    )(pb3, pb3, pb3)


def _outproj_body(x_ref, ya_ref, yb_ref, yc_ref, yd_ref, gc_ref, gd_ref, w_ref, o_ref):
    yc = _rms(yc_ref[...], gc_ref[...]).astype(BF16)
    yd = _rms(yd_ref[...], gd_ref[...]).astype(BF16)
    w = GROUP_WIDTH
    acc = _dot(ya_ref[...], w_ref[0:w, :])
    acc += _dot(yb_ref[...], w_ref[w:2 * w, :])
    acc += _dot(yc, w_ref[2 * w:3 * w, :])
    acc += _dot(yd, w_ref[3 * w:4 * w, :])
    o_ref[...] = x_ref[...] + acc


def _outproj(x, ya, yb, yc, yd, gain_c, gain_d, w_out, tm=512):
    m, d = x.shape
    w = GROUP_WIDTH
    yblk = pl.BlockSpec((tm, w), lambda i: (i, 0))
    vec = pl.BlockSpec((1, w), lambda i: (0, 0))
    return pl.pallas_call(
        _outproj_body,
        grid=(m // tm,),
        in_specs=[pl.BlockSpec((tm, d), lambda i: (i, 0)), yblk, yblk, yblk, yblk, vec, vec,
                  pl.BlockSpec((4 * w, d), lambda i: (0, 0))],
        out_specs=pl.BlockSpec((tm, d), lambda i: (i, 0)),
        out_shape=jax.ShapeDtypeStruct((m, d), F32),
        compiler_params=_params(("parallel",)),
        name="outproj",
    )(x, ya, yb, yc, yd, gain_c, gain_d, w_out)


def _ffn_weights(w_gate, w_up, w_down):
    padc = ((0, 0), (0, 0), (0, D_FF_PAD - D_FF))
    padr = ((0, 0), (0, D_FF_PAD - D_FF), (0, 0))
    return (jnp.pad(w_gate, padc).astype(BF16), jnp.pad(w_up, padc).astype(BF16),
            jnp.pad(w_down, padr).astype(BF16))


def _row(v):
    return v.reshape(1, -1).astype(F32)


def _split_w_in(wl):
    w = GROUP_WIDTH
    gate_lo = 6 * w
    gate_hi = gate_lo + 2 * N_HEADS
    w_f = jnp.concatenate([wl[:, 0:2 * w], wl[:, 5 * w:6 * w], wl[:, gate_hi:gate_hi + 3 * w]], axis=1)
    w_b = jnp.concatenate([wl[:, 2 * w:5 * w], wl[:, gate_hi + 3 * w:]], axis=1)
    return w_f.astype(BF16), w_b.astype(BF16), wl[:, gate_lo:gate_hi].astype(BF16)


def kernel(x, ffn1_norm, ffn1_w_gate, ffn1_w_up, ffn1_w_down, mix_norm, w_in, lru_conv_w, lru_conv_b, lru_w_a, lru_b_a, lru_w_x, lru_b_x, lru_lambda, mlstm_ig_bias, mlstm_fg_bias, attn_q_gain, attn_k_gain, group_out_gain, w_out, ffn2_norm, ffn2_w_gate, ffn2_w_up, ffn2_w_down):
    b, s, d = x.shape
    depth = w_in.shape[0]
    m = b * s
    w = GROUP_WIDTH
    slopes = 2.0 ** (-8.0 * jnp.arange(1, N_HEADS + 1, dtype=F32) / N_HEADS)
    slopes = jnp.broadcast_to(slopes[:, None, None], (N_HEADS, 8, HEAD_DIM))
    col_scale = jnp.ones((6, w), F32).at[PB_MK].set(ATTN_SCALE).at[PB_SQ].set(SB_Q_SCALE).reshape(1, D_HALF)

    ffn1_w = _ffn_weights(ffn1_w_gate, ffn1_w_up, ffn1_w_down)
    ffn2_w = _ffn_weights(ffn2_w_gate, ffn2_w_up, ffn2_w_down)

    xf = x.reshape(m, d)
    for l in range(depth):
        xf = _ffn(xf, _row(ffn1_norm[l]), ffn1_w, l)

        w_f, w_b, w_gate = _split_w_in(w_in[l])
        w_gate_c = jnp.pad(w_gate, ((0, 0), (0, GATE_LANES - 2 * N_HEADS)))
        w_gate_r = jnp.pad(w_gate.T, ((0, GATE_ROWS - 2 * N_HEADS), (0, 0)))
        pf, pb, gates_c, gates_r = _inproj(xf, _row(mix_norm[l]), w_f, w_b, col_scale, w_gate_c, w_gate_r)
        pf3 = pf.reshape(b, s, D_HALF)
        pb3 = pb.reshape(b, s, D_HALF)

        gains = group_out_gain[l].reshape(4, 1, w)
        ya = _lru(pf3, lru_conv_w[l], _row(lru_conv_b[l]), lru_w_a[l].astype(BF16), _row(lru_b_a[l]),
                  lru_w_x[l].astype(BF16), _row(lru_b_x[l]), _row(lru_lambda[l]), gains[0])

        gate_bias = jnp.concatenate([mlstm_ig_bias[l], mlstm_fg_bias[l]]).astype(F32)
        bias_c = jnp.pad(gate_bias, (0, GATE_LANES - 2 * N_HEADS)).reshape(1, GATE_LANES)
        bias_r = jnp.pad(gate_bias, (0, GATE_ROWS - 2 * N_HEADS)).reshape(GATE_ROWS, 1)
        yb = _mlstm(pf3, pb3, gates_c, gates_r, bias_c, bias_r, gains[1])

        yc = _dilated(pf3, _row(attn_q_gain[l]), _row(attn_k_gain[l]), slopes)
        yd = _stick_breaking(pb3)

        xf = _outproj(xf, ya.reshape(m, w), yb.reshape(m, w), yc.reshape(m, w), yd.reshape(m, w),
                      gains[2], gains[3], w_out[l].astype(BF16))

        xf = _ffn(xf, _row(ffn2_norm[l]), ffn2_w, l)
    return xf.reshape(b, s, d)
```

```python
import jax
import jax.numpy as jnp
from jax import lax
from jax.experimental import pallas as pl
from jax.experimental.pallas import tpu as pltpu

F32 = jnp.float32
BF16 = jnp.bfloat16

D_MODEL = 2048
N_HEADS = 4
HEAD_DIM = 128
GROUP_WIDTH = 512
D_FF = 5504
FF_BLK = 128
FF_SUB = 4
FF_TILE = FF_SUB * FF_BLK
LRU_C = 8.0
LRU_CONV_WIDTH = 4
DILATED_PATTERNS = ((128, 1), (512, 4), (2048, 16))
BLK = 128
SB_BLK = 256
DIL_PAR = 4
RMS_EPS = 1e-6
NEG_BIG = -1e30
ATTN_SCALE = HEAD_DIM ** -0.5
SB_Q_SCALE = ATTN_SCALE * 1.4426950408889634
GATE_LANES = 128
GATE_ROWS = 16
VMEM_LIMIT = 52 * 1024 * 1024
FFN_VMEM_LIMIT = 58 * 1024 * 1024

PF_LX, PF_LG, PF_MO, PF_CQ, PF_CK, PF_CV = range(6)
PB_MQ, PB_MK, PB_MV, PB_SQ, PB_SK, PB_SV = range(6)
D_HALF = 6 * GROUP_WIDTH
PROJ_TILE = 1024

_NT = (((1,), (1,)), ((), ()))
_TN = (((0,), (0,)), ((), ()))


def _rms(x, gain):
    return x * lax.rsqrt(jnp.mean(x * x, axis=-1, keepdims=True) + RMS_EPS) * gain


def _softplus(x):
    return jnp.maximum(x, 0.0) + jnp.log(1.0 + jnp.exp(-jnp.abs(x)))


def _log_sigmoid(x):
    return -_softplus(-x)


def _dot(a, b):
    return jnp.dot(a, b, preferred_element_type=F32)


def _dot_nt(a, b):
    return lax.dot_general(a, b, _NT, preferred_element_type=F32)


def _split_bf16(x, parts):
    out = []
    r = x
    for _ in range(parts):
        t = r.astype(BF16)
        out.append(t)
        r = r - t.astype(F32)
    return out


def _params(sem, vmem_limit=None):
    return pltpu.CompilerParams(dimension_semantics=sem, vmem_limit_bytes=vmem_limit or VMEM_LIMIT)


def _ffn_body(x_ref, g_ref, *refs):
    wg_refs, wu_refs, wd_refs = refs[:FF_SUB], refs[FF_SUB:2 * FF_SUB], refs[2 * FF_SUB:3 * FF_SUB]
    o_ref, h_ref = refs[3 * FF_SUB:]
    j = pl.program_id(1)

    @pl.when(j == 0)
    def _():
        h_ref[...] = _rms(x_ref[...], g_ref[...]).astype(BF16)
        o_ref[...] = jnp.zeros_like(o_ref)

    h = h_ref[...]
    g = _dot(h, jnp.concatenate([r[...] for r in wg_refs], axis=1))
    u = _dot(h, jnp.concatenate([r[...] for r in wu_refs], axis=1))
    col = lax.broadcasted_iota(jnp.int32, (1, FF_TILE), 1)
    a = jnp.where(col < D_FF - j * FF_TILE, (g * jax.nn.sigmoid(g)) * u, 0.0)
    o_ref[...] += _dot(a.astype(BF16), jnp.concatenate([r[...] for r in wd_refs], axis=0))

    @pl.when(j == pl.num_programs(1) - 1)
    def _():
        o_ref[...] = x_ref[...] + 0.5 * o_ref[...]


def _ffn(x, gain, weights, layer, tm=1024):
    wg, wu, wd = weights
    m, d = x.shape
    last_blk = D_FF // FF_BLK - 1

    def blk(k):
        return lambda i, j: jnp.minimum(j * FF_SUB + k, last_blk)

    cols = [pl.BlockSpec((None, d, FF_BLK), lambda i, j, f=blk(k): (layer, 0, f(i, j))) for k in range(FF_SUB)]
    rows = [pl.BlockSpec((None, FF_BLK, d), lambda i, j, f=blk(k): (layer, f(i, j), 0)) for k in range(FF_SUB)]
    return pl.pallas_call(
        _ffn_body,
        grid=(m // tm, pl.cdiv(D_FF, FF_TILE)),
        in_specs=[pl.BlockSpec((tm, d), lambda i, j: (i, 0)),
                  pl.BlockSpec((1, d), lambda i, j: (0, 0))] + cols + cols + rows,
        out_specs=pl.BlockSpec((tm, d), lambda i, j: (i, 0)),
        out_shape=jax.ShapeDtypeStruct((m, d), F32),
        scratch_shapes=[pltpu.VMEM((tm, d), BF16)],
        compiler_params=_params(("parallel", "arbitrary"), FFN_VMEM_LIMIT),
        name="ffn",
    )(x, gain, *([wg] * FF_SUB), *([wu] * FF_SUB), *([wd] * FF_SUB))


def _inproj_body(x_ref, g_ref, wf_ref, wb_ref, sc_ref, wgc_ref,
                 pf_ref, pb_ref, gc_ref, gr_ref, h_ref):
    j = pl.program_id(1)

    @pl.when(j == 0)
    def _():
        h = _rms(x_ref[...], g_ref[...]).astype(BF16)
        h_ref[...] = h
        gates = _dot(h, wgc_ref[...])
        gc_ref[...] = gates
        gr_ref[...] = gates.T[:GATE_ROWS, :]

    h = h_ref[...]
    pf_ref[...] = _dot(h, wf_ref[...])
    pb_ref[...] = (_dot(h, wb_ref[...]) * sc_ref[...]).astype(BF16)


def _inproj(x, gain, w_f, w_b, col_scale, w_gate_c, layer, tm=512):
    m, d = x.shape
    tn = PROJ_TILE
    return pl.pallas_call(
        _inproj_body,
        grid=(m // tm, D_HALF // tn),
        in_specs=[
            pl.BlockSpec((tm, d), lambda i, j: (i, 0)),
            pl.BlockSpec((1, d), lambda i, j: (0, 0)),
            pl.BlockSpec((None, d, tn), lambda i, j: (layer, 0, j)),
            pl.BlockSpec((None, d, tn), lambda i, j: (layer, 0, j)),
            pl.BlockSpec((1, tn), lambda i, j: (0, j)),
            pl.BlockSpec((None, d, GATE_LANES), lambda i, j: (layer, 0, 0)),
        ],
        out_specs=[
            pl.BlockSpec((tm, tn), lambda i, j: (i, j)),
            pl.BlockSpec((tm, tn), lambda i, j: (i, j)),
            pl.BlockSpec((tm, GATE_LANES), lambda i, j: (i, 0)),
            pl.BlockSpec((GATE_ROWS, tm), lambda i, j: (0, i)),
        ],
        out_shape=[
            jax.ShapeDtypeStruct((m, D_HALF), F32),
            jax.ShapeDtypeStruct((m, D_HALF), BF16),
            jax.ShapeDtypeStruct((m, GATE_LANES), F32),
            jax.ShapeDtypeStruct((GATE_ROWS, m), F32),
        ],
        scratch_shapes=[pltpu.VMEM((tm, d), BF16)],
        compiler_params=_params(("parallel", "arbitrary")),
        name="inproj",
    )(x, gain, w_f, w_b, col_scale, w_gate_c)


def _lru_body(xr_ref, gate_ref, cw_ref, cb_ref, wa_ref, ba_ref, wx_ref, bx_ref, lam_ref,
              gain_ref, o_ref, xbuf, hcar):
    t = pl.program_id(1)
    tt = xr_ref.shape[1]
    width = xr_ref.shape[2]
    pad = 8

    @pl.when(t == 0)
    def _():
        xbuf[0:pad, :] = jnp.zeros((pad, width), F32)
        hcar[...] = jnp.zeros_like(hcar)

    xr = xr_ref[0]
    xbuf[pad:pad + tt, :] = xr
    xc = cb_ref[...]
    for j in range(LRU_CONV_WIDTH):
        off = pad - (LRU_CONV_WIDTH - 1) + j
        xc = xc + cw_ref[j:j + 1, :] * xbuf[pl.ds(off, tt), :]
    xbuf[0:pad, :] = xr[tt - pad:tt, :]

    xcb = xc.astype(BF16)
    ra, rx = [], []
    for n in range(N_HEADS):
        blk = xcb[:, n * HEAD_DIM:(n + 1) * HEAD_DIM]
        ra.append(_dot(blk, wa_ref[n]))
        rx.append(_dot(blk, wx_ref[n]))
    r = jax.nn.sigmoid(jnp.concatenate(ra, axis=1) + ba_ref[...])
    i = jax.nn.sigmoid(jnp.concatenate(rx, axis=1) + bx_ref[...])
    log_a = (-LRU_C * _softplus(-lam_ref[...])) * r
    a = jnp.exp(log_a)
    u = jnp.sqrt(-jnp.tanh(log_a) * (a * a + 1.0)) * (i * xc)

    row = lax.broadcasted_iota(jnp.int32, (tt, width), 0)
    s = 1
    while s < tt:
        keep = row >= s
        a_sh = jnp.where(keep, pltpu.roll(a, s, 0), 1.0)
        u_sh = jnp.where(keep, pltpu.roll(u, s, 0), 0.0)
        u = a * u_sh + u
        a = a * a_sh
        s *= 2
    h = u + a * hcar[...]
    hcar[...] = h[tt - 1:tt, :]

    y = h * jax.nn.gelu(gate_ref[0])
    o_ref[0] = _rms(y, gain_ref[...]).astype(BF16)


def _lru(pf3, conv_w, conv_b, w_a, b_a, w_x, b_x, lam, gain, tt=512):
    b, s, _ = pf3.shape
    w = GROUP_WIDTH
    vec = pl.BlockSpec((1, w), lambda bi, ti: (0, 0))
    mat = pl.BlockSpec((N_HEADS, HEAD_DIM, HEAD_DIM), lambda bi, ti: (0, 0, 0))
    return pl.pallas_call(
        _lru_body,
        grid=(b, s // tt),
        in_specs=[
            pl.BlockSpec((1, tt, w), lambda bi, ti: (bi, ti, PF_LX)),
            pl.BlockSpec((1, tt, w), lambda bi, ti: (bi, ti, PF_LG)),
            pl.BlockSpec((LRU_CONV_WIDTH, w), lambda bi, ti: (0, 0)),
            vec, mat, vec, mat, vec, vec, vec,
        ],
        out_specs=pl.BlockSpec((1, tt, w), lambda bi, ti: (bi, ti, 0)),
        out_shape=jax.ShapeDtypeStruct((b, s, w), BF16),
        scratch_shapes=[pltpu.VMEM((tt + 8, w), F32), pltpu.VMEM((1, w), F32)],
        compiler_params=_params(("parallel", "arbitrary")),
        name="rglru",
    )(pf3, pf3, conv_w, conv_b, w_a, b_a, w_x, b_x, lam, gain)


def _mlstm_body(q_ref, k_ref, v_ref, og_ref, gc_ref, gr_ref, bc_ref, br_ref, hg_ref,
                out_ref, c_ref, m_ref):
    c = pl.program_id(1)
    ln = q_ref.shape[1]

    @pl.when(c == 0)
    def _():
        c_ref[...] = jnp.zeros_like(c_ref)
        m_ref[...] = jnp.zeros_like(m_ref)

    gcol = gc_ref[0] + bc_ref[...]
    grow = gr_ref[...] + br_ref[...]
    lf_col = _log_sigmoid(gcol)
    lf_row = _log_sigmoid(grow)
    ri = lax.broadcasted_iota(jnp.int32, (ln, ln), 0)
    ci = lax.broadcasted_iota(jnp.int32, (ln, ln), 1)
    causal = ri >= ci
    tri_l = jnp.where(causal, 1.0, 0.0).astype(BF16)
    tri_u = jnp.where(ri <= ci, 1.0, 0.0).astype(BF16)
    b_col = sum(_dot(tri_l, part) for part in _split_bf16(lf_col, 3))
    b_row = sum(_dot(part, tri_u) for part in _split_bf16(lf_row, 3))

    ones_blk = jnp.ones((ln, HEAD_DIM), BF16)
    reps = ln // HEAD_DIM

    def wide(x):
        return jnp.concatenate([x] * reps, axis=1)

    heads = [slice(h * HEAD_DIM, (h + 1) * HEAD_DIM) for h in range(N_HEADS)]
    qs = [q_ref[0, :, sl] for sl in heads]
    ks = [k_ref[0, :, sl] for sl in heads]
    v_augs = [jnp.concatenate([v_ref[0, :, sl], ones_blk], axis=1) for sl in heads]
    c_augs = [c_ref[h] for h in range(N_HEADS)]

    decay_w, m_ts, w_inters, m_nexts, decays, kws = [], [], [], [], [], []
    for h in range(N_HEADS):
        bc = jnp.broadcast_to(b_col[:, N_HEADS + h:N_HEADS + h + 1], (ln, HEAD_DIM))
        ig = jnp.broadcast_to(gcol[:, h:h + 1], (ln, HEAD_DIM))
        brow = b_row[N_HEADS + h:N_HEADS + h + 1, :]
        igr = grow[h:h + 1, :]
        m_run = m_ref[h:h + 1, :]
        b_last = bc[ln - 1:ln, :]
        log_d = jnp.where(causal, wide(bc) - brow + igr, NEG_BIG)
        inter = bc + m_run
        m_t = jnp.maximum(inter, jnp.max(log_d, axis=1, keepdims=True))
        decay_w.append(jnp.exp(log_d - wide(m_t)))
        m_ts.append(m_t)
        w_inters.append(jnp.exp(inter - m_t))
        log_w = b_last - bc + ig
        m_next = jnp.maximum(b_last + m_run, jnp.max(log_w, axis=0, keepdims=True))
        m_nexts.append(m_next)
        decays.append(jnp.exp(b_last + m_run - m_next))
        kws.append((ks[h].astype(F32) * jnp.exp(log_w - m_next)).astype(BF16))

    qk = [_dot_nt(q, k) for q, k in zip(qs, ks)]
    qc = [_dot(q, c_aug.astype(BF16)) for q, c_aug in zip(qs, c_augs)]
    smats = [(s * d).astype(BF16) for s, d in zip(qk, decay_w)]
    intra = [_dot(s, v_aug) for s, v_aug in zip(smats, v_augs)]
    upd = [lax.dot_general(kw, v_aug, _TN, preferred_element_type=F32) for kw, v_aug in zip(kws, v_augs)]

    for h, sl in enumerate(heads):
        num = intra[h][:, :HEAD_DIM] + w_inters[h] * qc[h][:, :HEAD_DIM]
        den = intra[h][:, HEAD_DIM:] + w_inters[h] * qc[h][:, HEAD_DIM:]
        hh = num / jnp.maximum(jnp.abs(den), jnp.exp(-m_ts[h]))
        c_ref[h] = jnp.concatenate([decays[h]] * 2, axis=1) * c_augs[h] + upd[h]
        m_ref[h:h + 1, :] = m_nexts[h]
        hn = _rms(hh, hg_ref[:, sl])
        out_ref[0, :, sl] = (hn * jax.nn.sigmoid(og_ref[0, :, sl])).astype(BF16)


def _mlstm(pf3, pb3, gates_c, gates_r, bias_c, bias_r, head_gain, ln=256):
    b, s, _ = pf3.shape
    nc = s // ln
    w = GROUP_WIDTH

    def col(g):
        return pl.BlockSpec((1, ln, w), lambda bi, ci, g=g: (bi, ci, g))

    return pl.pallas_call(
        _mlstm_body,
        grid=(b, nc),
        in_specs=[
            col(PB_MQ), col(PB_MK), col(PB_MV), col(PF_MO),
            pl.BlockSpec((1, ln, GATE_LANES), lambda bi, ci: (bi, ci, 0)),
            pl.BlockSpec((GATE_ROWS, ln), lambda bi, ci: (0, bi * nc + ci)),
            pl.BlockSpec((1, GATE_LANES), lambda bi, ci: (0, 0)),
            pl.BlockSpec((GATE_ROWS, 1), lambda bi, ci: (0, 0)),
            pl.BlockSpec((1, w), lambda bi, ci: (0, 0)),
        ],
        out_specs=pl.BlockSpec((1, ln, w), lambda bi, ci: (bi, ci, 0)),
        out_shape=jax.ShapeDtypeStruct((b, s, w), BF16),
        scratch_shapes=[pltpu.VMEM((N_HEADS, HEAD_DIM, 2 * HEAD_DIM), F32),
                        pltpu.VMEM((8, 128), F32)],
        compiler_params=_params(("parallel", "arbitrary")),
        name="mlstm",
    )(pb3, pb3, pb3, pf3, gates_c.reshape(b, s, GATE_LANES), gates_r, bias_c, bias_r, head_gain)


def _dil_body(q_ref, k_ref, v_ref, qg_ref, kg_ref, sl_ref, o_ref, qn, kn, m_s, l_s):
    s = q_ref.shape[1]
    n_blocks = s // BLK
    vv, o2 = v_ref.at[0], o_ref.at[0]
    qn[...] = _rms(q_ref[0], qg_ref[...]) * ATTN_SCALE
    kn[...] = _rms(k_ref[0], kg_ref[...])
    slope = sl_ref[0, 0:1, 0:1]
    qq = lax.broadcasted_iota(jnp.int32, (BLK, 2 * BLK), 0)
    kk = lax.broadcasted_iota(jnp.int32, (BLK, 2 * BLK), 1)
    dist = jnp.where(kk < BLK, qq - kk, qq - kk + 2 * BLK)
    in_window = jnp.logical_and(dist >= 0, dist <= BLK)
    prev_lanes = lax.broadcasted_iota(jnp.int32, (1, 2 * BLK), 1) >= BLK
    ones = jnp.ones((2 * BLK, HEAD_DIM), BF16)
    assert n_blocks % DIL_PAR == 0

    for pi, (window, dil) in enumerate(reversed(DILATED_PATTERNS)):
        assert window // dil == BLK and s % (dil * BLK) == 0
        nb = s // (dil * BLK)
        assert nb % DIL_PAR == 0 or DIL_PAR % nb == 0
        bias = jnp.where(in_window, (-float(dil) * slope) * dist.astype(F32), NEG_BIG)
        first, last = pi == 0, pi == len(DILATED_PATTERNS) - 1

        def rows(start, dil=dil):
            return pl.ds(start, BLK) if dil == 1 else pl.ds(start, BLK, stride=dil)

        def group(t0, carry, dil=dil, nb=nb, bias=bias, first=first, last=last, rows=rows):
            cur, qb, kc, vc, no_prev = [], [], [], [], []
            for i in range(DIL_PAR):
                t = t0 * DIL_PAR + i
                r = t // nb
                n = t - r * nb
                c = rows(r + dil * BLK * n)
                cur.append(c)
                qb.append(qn[c, :].astype(BF16))
                kc.append(kn[c, :].astype(BF16))
                vc.append(vv[c, :].astype(BF16))
                no_prev.append(jnp.where(prev_lanes, jnp.where(n > 0, 0.0, NEG_BIG), 0.0))
                if i == 0:
                    p = rows(r + dil * BLK * jnp.maximum(n - 1, 0))
                    kp, vp = [kn[p, :].astype(BF16)], [vv[p, :].astype(BF16)]
                elif i % nb == 0:
                    kp.append(kc[i])
                    vp.append(vc[i])
                else:
                    kp.append(kc[i - 1])
                    vp.append(vc[i - 1])
            k2 = [jnp.concatenate([a, b], axis=0) for a, b in zip(kc, kp)]
            v2 = [jnp.concatenate([a, b], axis=0) for a, b in zip(vc, vp)]
            sc = [jnp.maximum(_dot_nt(q, k) + bias + off, NEG_BIG) for q, k, off in zip(qb, k2, no_prev)]
            m_b = [jnp.max(x, axis=1, keepdims=True) for x in sc]
            pr = [jnp.exp(x - m).astype(BF16) for x, m in zip(sc, m_b)]
            nd = [_dot(p, jnp.concatenate([v, ones], axis=1)) for p, v in zip(pr, v2)]
            res = []
            for c, m, x in zip(cur, m_b, nd):
                num, den = x[:, :HEAD_DIM], x[:, HEAD_DIM:]
                if first:
                    res.append((jnp.broadcast_to(m, (BLK, HEAD_DIM)), den, num))
                else:
                    m_o = m_s[c, :]
                    m_n = jnp.maximum(m_o, m)
                    a_o = jnp.exp(m_o - m_n)
                    a_b = jnp.exp(m - m_n)
                    res.append((m_n, l_s[c, :] * a_o + den * a_b, o2[c, :] * a_o + num * a_b))
            for c, (m_n, l_n, acc) in zip(cur, res):
                if last:
                    o2[c, :] = acc / l_n
                else:
                    m_s[c, :] = m_n
                    l_s[c, :] = l_n
                    o2[c, :] = acc
            return carry

        lax.fori_loop(0, n_blocks // DIL_PAR, group, 0)


def _dilated(pf3, q_gain, k_gain, slopes):
    b, s, _ = pf3.shape

    def col(g):
        return pl.BlockSpec((1, s, HEAD_DIM), lambda bi, hi, g=g: (bi, 0, g * N_HEADS + hi))

    vec = pl.BlockSpec((1, HEAD_DIM), lambda bi, hi: (0, 0))
    return pl.pallas_call(
        _dil_body,
        grid=(b, N_HEADS),
        in_specs=[col(PF_CQ), col(PF_CK), col(PF_CV), vec, vec,
                  pl.BlockSpec((1, 8, HEAD_DIM), lambda bi, hi: (hi, 0, 0))],
        out_specs=pl.BlockSpec((1, s, HEAD_DIM), lambda bi, hi: (bi, 0, hi)),
        out_shape=jax.ShapeDtypeStruct((b, s, GROUP_WIDTH), F32),
        scratch_shapes=[pltpu.VMEM((s, HEAD_DIM), F32)] * 4,
        compiler_params=_params(("parallel", "parallel")),
        name="dilated",
    )(pf3, pf3, pf3, q_gain, k_gain, slopes)


def _sb_body(q_ref, k_ref, v_ref, o_ref, z_ref):
    i = pl.program_id(1)
    nq = SB_BLK
    qq = lax.broadcasted_iota(jnp.int32, (nq, nq), 0)
    kk = lax.broadcasted_iota(jnp.int32, (nq, nq), 1)
    strict = kk < qq
    after = jnp.where(qq > kk, 1.0, 0.0).astype(BF16)
    after2 = jnp.concatenate([after, after], axis=0)
    o_ref[...] = jnp.zeros_like(o_ref)

    heads = [slice(h * HEAD_DIM, (h + 1) * HEAD_DIM) for h in range(N_HEADS)]

    def logits(j):
        keys = pl.ds(pl.multiple_of(j * nq, nq), nq)
        return [_dot_nt(q_ref[0, :, sl], k_ref[0, keys, sl]) for sl in heads]

    def step(j, gone, masked):
        keys = pl.ds(pl.multiple_of(j * nq, nq), nq)
        zs = [z_ref[h] for h in range(N_HEADS)]
        sps = [jnp.maximum(z, 0.0) + jnp.log2(1.0 + jnp.exp2(-jnp.abs(z))) for z in zs]
        log_beta = [z - sp for z, sp in zip(zs, sps)]
        drops = [jnp.where(strict, sp, 0.0) for sp in sps] if masked else sps
        gone_next = tuple(g + jnp.sum(drop, axis=1, keepdims=True) for g, drop in zip(gone, drops))
        for h, z in enumerate(logits(jnp.maximum(j - 1, 0))):
            z_ref[h] = z
        parts = [jnp.concatenate(_split_bf16(drop, 2), axis=1) for drop in drops]
        laters = [_dot(part, after2) for part in parts]
        ws = [jnp.exp2(lb - later - g) for lb, later, g in zip(log_beta, laters, gone)]
        if masked:
            ws = [jnp.where(strict, w, 0.0) for w in ws]
        pv = [_dot(w.astype(BF16), v_ref[0, keys, sl]) for w, sl in zip(ws, heads)]
        o_ref[0] += jnp.concatenate(pv, axis=1)
        return gone_next

    for h, z in enumerate(logits(i)):
        z_ref[h] = z
    gone = step(i, tuple(jnp.zeros((nq, 1), F32) for _ in range(N_HEADS)), True)
    lax.fori_loop(0, i, lambda jj, g: step(i - 1 - jj, g, False), gone)


def _stick_breaking(pb3):
    b, s, _ = pb3.shape
    w = GROUP_WIDTH
    return pl.pallas_call(
        _sb_body,
        grid=(b, s // SB_BLK),
        in_specs=[
            pl.BlockSpec((1, SB_BLK, w), lambda bi, qi: (bi, qi, PB_SQ)),
            pl.BlockSpec((1, s, w), lambda bi, qi: (bi, 0, PB_SK)),
            pl.BlockSpec((1, s, w), lambda bi, qi: (bi, 0, PB_SV)),
        ],
        out_specs=pl.BlockSpec((1, SB_BLK, w), lambda bi, qi: (bi, qi, 0)),
        out_shape=jax.ShapeDtypeStruct((b, s, w), F32),
        scratch_shapes=[pltpu.VMEM((N_HEADS, SB_BLK, SB_BLK), F32)],
        compiler_params=_params(("parallel", "parallel")),
        name="stick_breaking",
    )(pb3, pb3, pb3)


def _outproj_body(x_ref, ya_ref, yb_ref, yc_ref, yd_ref, gc_ref, gd_ref, w_ref, o_ref):
    yc = _rms(yc_ref[...], gc_ref[...]).astype(BF16)
    yd = _rms(yd_ref[...], gd_ref[...]).astype(BF16)
    w = GROUP_WIDTH
    acc = _dot(ya_ref[...], w_ref[0:w, :])
    acc += _dot(yb_ref[...], w_ref[w:2 * w, :])
    acc += _dot(yc, w_ref[2 * w:3 * w, :])
    acc += _dot(yd, w_ref[3 * w:4 * w, :])
    o_ref[...] = x_ref[...] + acc


def _outproj(x, ya, yb, yc, yd, gain_c, gain_d, w_out, layer, tm=512):
    m, d = x.shape
    w = GROUP_WIDTH
    yblk = pl.BlockSpec((tm, w), lambda i: (i, 0))
    vec = pl.BlockSpec((1, w), lambda i: (0, 0))
    return pl.pallas_call(
        _outproj_body,
        grid=(m // tm,),
        in_specs=[pl.BlockSpec((tm, d), lambda i: (i, 0)), yblk, yblk, yblk, yblk, vec, vec,
                  pl.BlockSpec((None, 4 * w, d), lambda i: (layer, 0, 0))],
        out_specs=pl.BlockSpec((tm, d), lambda i: (i, 0)),
        out_shape=jax.ShapeDtypeStruct((m, d), F32),
        compiler_params=_params(("parallel",)),
        name="outproj",
    )(x, ya, yb, yc, yd, gain_c, gain_d, w_out)


def _row(v):
    return v.reshape(1, -1).astype(F32)


def _split_w_in(w_in):
    w = GROUP_WIDTH
    gate_lo = 6 * w
    gate_hi = gate_lo + 2 * N_HEADS
    w_f = jnp.concatenate([w_in[..., 0:2 * w], w_in[..., 5 * w:6 * w], w_in[..., gate_hi:gate_hi + 3 * w]], axis=-1)
    w_b = jnp.concatenate([w_in[..., 2 * w:5 * w], w_in[..., gate_hi + 3 * w:]], axis=-1)
    w_gate = jnp.pad(w_in[..., gate_lo:gate_hi], ((0, 0), (0, 0), (0, GATE_LANES - 2 * N_HEADS)))
    return w_f.astype(BF16), w_b.astype(BF16), w_gate.astype(BF16)


def kernel(x, ffn1_norm, ffn1_w_gate, ffn1_w_up, ffn1_w_down, mix_norm, w_in, lru_conv_w, lru_conv_b, lru_w_a, lru_b_a, lru_w_x, lru_b_x, lru_lambda, mlstm_ig_bias, mlstm_fg_bias, attn_q_gain, attn_k_gain, group_out_gain, w_out, ffn2_norm, ffn2_w_gate, ffn2_w_up, ffn2_w_down):
    b, s, d = x.shape
    depth = w_in.shape[0]
    m = b * s
    w = GROUP_WIDTH
    slopes = 2.0 ** (-8.0 * jnp.arange(1, N_HEADS + 1, dtype=F32) / N_HEADS)
    slopes = jnp.broadcast_to(slopes[:, None, None], (N_HEADS, 8, HEAD_DIM))
    col_scale = jnp.ones((6, w), F32).at[PB_MK].set(ATTN_SCALE).at[PB_SQ].set(SB_Q_SCALE).reshape(1, D_HALF)

    ffn1_w = tuple(t.astype(BF16) for t in (ffn1_w_gate, ffn1_w_up, ffn1_w_down))
    ffn2_w = tuple(t.astype(BF16) for t in (ffn2_w_gate, ffn2_w_up, ffn2_w_down))
    w_f, w_b, w_gate = _split_w_in(w_in)
    w_out_b = w_out.astype(BF16)

    xf = x.reshape(m, d)
    for l in range(depth):
        xf = _ffn(xf, _row(ffn1_norm[l]), ffn1_w, l)

        pf, pb, gates_c, gates_r = _inproj(xf, _row(mix_norm[l]), w_f, w_b, col_scale, w_gate, l)
        pf3 = pf.reshape(b, s, D_HALF)
        pb3 = pb.reshape(b, s, D_HALF)

        gains = group_out_gain[l].reshape(4, 1, w)
        ya = _lru(pf3, lru_conv_w[l], _row(lru_conv_b[l]), lru_w_a[l].astype(BF16), _row(lru_b_a[l]),
                  lru_w_x[l].astype(BF16), _row(lru_b_x[l]), _row(lru_lambda[l]), gains[0])

        gate_bias = jnp.concatenate([mlstm_ig_bias[l], mlstm_fg_bias[l]]).astype(F32)
        bias_c = jnp.pad(gate_bias, (0, GATE_LANES - 2 * N_HEADS)).reshape(1, GATE_LANES)
        bias_r = jnp.pad(gate_bias, (0, GATE_ROWS - 2 * N_HEADS)).reshape(GATE_ROWS, 1)
        yb = _mlstm(pf3, pb3, gates_c, gates_r, bias_c, bias_r, gains[1])

        yc = _dilated(pf3, _row(attn_q_gain[l]), _row(attn_k_gain[l]), slopes)
        yd = _stick_breaking(pb3)

        xf = _outproj(xf, ya.reshape(m, w), yb.reshape(m, w), yc.reshape(m, w), yd.reshape(m, w),
                      gains[2], gains[3], w_out_b, l)

        xf = _ffn(xf, _row(ffn2_norm[l]), ffn2_w, l)
    return xf.reshape(b, s, d)
```

```python
import jax
import jax.numpy as jnp
from jax import lax
from jax.experimental import pallas as pl
from jax.experimental.pallas import tpu as pltpu

F32 = jnp.float32
BF16 = jnp.bfloat16

D_MODEL = 2048
N_HEADS = 4
HEAD_DIM = 128
GROUP_WIDTH = 512
D_FF = 5504
FF_BLK = 128
FF_SUB = 4
FF_TILE = FF_SUB * FF_BLK
LRU_C = 8.0
LRU_CONV_WIDTH = 4
DILATED_PATTERNS = ((128, 1), (512, 4), (2048, 16))
BLK = 128
SB_BLK = 256
DIL_PAR = 4
RMS_EPS = 1e-6
NEG_BIG = -1e30
ATTN_SCALE = HEAD_DIM ** -0.5
SB_Q_SCALE = ATTN_SCALE * 1.4426950408889634
GATE_LANES = 128
GATE_ROWS = 16
SUBLANES = 8
BF16_ROWS = 16
CAST_STEPS = 8
VMEM_LIMIT = 52 * 1024 * 1024
FFN_VMEM_LIMIT = 58 * 1024 * 1024

PF_LX, PF_LG, PF_MO, PF_CQ, PF_CK, PF_CV = range(6)
PB_MQ, PB_MK, PB_MV, PB_SQ, PB_SK, PB_SV = range(6)
D_HALF = 6 * GROUP_WIDTH
PROJ_TILE = 512

_NT = (((1,), (1,)), ((), ()))
_TN = (((0,), (0,)), ((), ()))


def _rms(x, gain):
    return x * lax.rsqrt(jnp.mean(x * x, axis=-1, keepdims=True) + RMS_EPS) * gain


def _softplus(x):
    return jnp.maximum(x, 0.0) + jnp.log(1.0 + jnp.exp(-jnp.abs(x)))


def _log_sigmoid(x):
    return -_softplus(-x)


def _dot(a, b):
    return jnp.dot(a, b, preferred_element_type=F32)


def _dot_nt(a, b):
    return lax.dot_general(a, b, _NT, preferred_element_type=F32)


def _split_bf16(x, parts):
    out = []
    r = x
    for _ in range(parts):
        t = r.astype(BF16)
        out.append(t)
        r = r - t.astype(F32)
    return out


def _params(sem, vmem_limit=None):
    return pltpu.CompilerParams(dimension_semantics=sem, vmem_limit_bytes=vmem_limit or VMEM_LIMIT)


def _ffn_body(x_ref, g_ref, *refs):
    wg_refs, wu_refs, wd_refs = refs[:FF_SUB], refs[FF_SUB:2 * FF_SUB], refs[2 * FF_SUB:3 * FF_SUB]
    o_ref, h_ref = refs[3 * FF_SUB:]
    j = pl.program_id(1)

    @pl.when(j == 0)
    def _():
        h_ref[...] = _rms(x_ref[...], g_ref[...]).astype(BF16)
        o_ref[...] = jnp.zeros_like(o_ref)

    h = h_ref[...]
    g = _dot(h, jnp.concatenate([r[...] for r in wg_refs], axis=1))
    u = _dot(h, jnp.concatenate([r[...] for r in wu_refs], axis=1))
    col = lax.broadcasted_iota(jnp.int32, (1, FF_TILE), 1)
    a = jnp.where(col < D_FF - j * FF_TILE, (g * jax.nn.sigmoid(g)) * u, 0.0)
    o_ref[...] += _dot(a.astype(BF16), jnp.concatenate([r[...] for r in wd_refs], axis=0))

    @pl.when(j == pl.num_programs(1) - 1)
    def _():
        o_ref[...] = x_ref[...] + 0.5 * o_ref[...]


def _ffn(x, gain, weights, tm=1024):
    wg, wu, wd = weights
    m, d = x.shape
    last_blk = D_FF // FF_BLK - 1

    def blk(k):
        return lambda i, j: jnp.minimum(j * FF_SUB + k, last_blk)

    cols = [pl.BlockSpec((d, FF_BLK), lambda i, j, f=blk(k): (0, f(i, j))) for k in range(FF_SUB)]
    rows = [pl.BlockSpec((FF_BLK, d), lambda i, j, f=blk(k): (f(i, j), 0)) for k in range(FF_SUB)]
    return pl.pallas_call(
        _ffn_body,
        grid=(m // tm, pl.cdiv(D_FF, FF_TILE)),
        in_specs=[pl.BlockSpec((tm, d), lambda i, j: (i, 0)),
                  pl.BlockSpec((1, d), lambda i, j: (0, 0))] + cols + cols + rows,
        out_specs=pl.BlockSpec((tm, d), lambda i, j: (i, 0)),
        out_shape=jax.ShapeDtypeStruct((m, d), F32),
        scratch_shapes=[pltpu.VMEM((tm, d), BF16)],
        compiler_params=_params(("parallel", "arbitrary"), FFN_VMEM_LIMIT),
        name="ffn",
    )(x, gain, *([wg] * FF_SUB), *([wu] * FF_SUB), *([wd] * FF_SUB))


def _inproj_body(x_ref, g_ref, wf_ref, wb_ref, sc_ref, wgc_ref,
                 pf_ref, pb_ref, gc_ref, gr_ref, h_ref):
    j = pl.program_id(1)

    @pl.when(j == 0)
    def _():
        h = _rms(x_ref[...], g_ref[...]).astype(BF16)
        h_ref[...] = h
        gates = _dot(h, wgc_ref[...])
        gc_ref[...] = gates
        gr_ref[...] = gates.T[:GATE_ROWS, :]

    h = h_ref[...]
    pf_ref[...] = _dot(h, wf_ref[...])
    pb_ref[...] = (_dot(h, wb_ref[...]) * sc_ref[...]).astype(BF16)


def _inproj(x, gain, w_f, w_b, col_scale, w_gate_c, layer, tm=1024):
    m, d = x.shape
    tn = PROJ_TILE
    return pl.pallas_call(
        _inproj_body,
        grid=(m // tm, D_HALF // tn),
        in_specs=[
            pl.BlockSpec((tm, d), lambda i, j: (i, 0)),
            pl.BlockSpec((1, d), lambda i, j: (0, 0)),
            pl.BlockSpec((None, d, tn), lambda i, j: (layer, 0, j)),
            pl.BlockSpec((None, d, tn), lambda i, j: (layer, 0, j)),
            pl.BlockSpec((1, tn), lambda i, j: (0, j)),
            pl.BlockSpec((None, d, GATE_LANES), lambda i, j: (layer, 0, 0)),
        ],
        out_specs=[
            pl.BlockSpec((tm, tn), lambda i, j: (i, j)),
            pl.BlockSpec((tm, tn), lambda i, j: (i, j)),
            pl.BlockSpec((tm, GATE_LANES), lambda i, j: (i, 0)),
            pl.BlockSpec((GATE_ROWS, tm), lambda i, j: (0, i)),
        ],
        out_shape=[
            jax.ShapeDtypeStruct((m, D_HALF), F32),
            jax.ShapeDtypeStruct((m, D_HALF), BF16),
            jax.ShapeDtypeStruct((m, GATE_LANES), F32),
            jax.ShapeDtypeStruct((GATE_ROWS, m), F32),
        ],
        scratch_shapes=[pltpu.VMEM((tm, d), BF16)],
        compiler_params=_params(("parallel", "arbitrary")),
        name="inproj",
    )(x, gain, w_f, w_b, col_scale, w_gate_c)


def _lru_body(xr_ref, gate_ref, cw_ref, cb_ref, wa_ref, ba_ref, wx_ref, bx_ref, lam_ref,
              gain_ref, o_ref, xbuf, hcar):
    t = pl.program_id(1)
    tt = xr_ref.shape[1]
    width = xr_ref.shape[2]
    pad = 8

    @pl.when(t == 0)
    def _():
        xbuf[0:pad, :] = jnp.zeros((pad, width), F32)
        hcar[...] = jnp.zeros_like(hcar)

    xr = xr_ref[0]
    xbuf[pad:pad + tt, :] = xr
    xc = cb_ref[...]
    for j in range(LRU_CONV_WIDTH):
        off = pad - (LRU_CONV_WIDTH - 1) + j
        xc = xc + cw_ref[j:j + 1, :] * xbuf[pl.ds(off, tt), :]
    xbuf[0:pad, :] = xr[tt - pad:tt, :]

    xcb = xc.astype(BF16)
    ra, rx = [], []
    for n in range(N_HEADS):
        blk = xcb[:, n * HEAD_DIM:(n + 1) * HEAD_DIM]
        ra.append(_dot(blk, wa_ref[n]))
        rx.append(_dot(blk, wx_ref[n]))
    r = jax.nn.sigmoid(jnp.concatenate(ra, axis=1) + ba_ref[...])
    i = jax.nn.sigmoid(jnp.concatenate(rx, axis=1) + bx_ref[...])
    log_a = (-LRU_C * _softplus(-lam_ref[...])) * r
    a = jnp.exp(log_a)
    u = jnp.sqrt(-jnp.tanh(log_a) * (a * a + 1.0)) * (i * xc)

    sub = lax.broadcasted_iota(jnp.int32, (tt, width), 0) % SUBLANES
    s = 1
    while s < SUBLANES:
        keep = sub >= s
        a_sh = jnp.where(keep, pltpu.roll(a, s, 0), 1.0)
        u_sh = jnp.where(keep, pltpu.roll(u, s, 0), 0.0)
        u = a * u_sh + u
        a = a * a_sh
        s *= 2
    carry = hcar[...]
    groups = []
    for g in range(tt // SUBLANES):
        rows = slice(g * SUBLANES, (g + 1) * SUBLANES)
        hg = u[rows, :] + a[rows, :] * carry
        groups.append(hg)
        carry = hg[SUBLANES - 1:SUBLANES, :]
    h = jnp.concatenate(groups, axis=0)
    hcar[...] = carry

    y = h * jax.nn.gelu(gate_ref[0])
    o_ref[0] = _rms(y, gain_ref[...]).astype(BF16)


def _lru(pf3, conv_w, conv_b, w_a, b_a, w_x, b_x, lam, gain, tt=512):
    b, s, _ = pf3.shape
    w = GROUP_WIDTH
    vec = pl.BlockSpec((1, w), lambda bi, ti: (0, 0))
    mat = pl.BlockSpec((N_HEADS, HEAD_DIM, HEAD_DIM), lambda bi, ti: (0, 0, 0))
    return pl.pallas_call(
        _lru_body,
        grid=(b, s // tt),
        in_specs=[
            pl.BlockSpec((1, tt, w), lambda bi, ti: (bi, ti, PF_LX)),
            pl.BlockSpec((1, tt, w), lambda bi, ti: (bi, ti, PF_LG)),
            pl.BlockSpec((LRU_CONV_WIDTH, w), lambda bi, ti: (0, 0)),
            vec, mat, vec, mat, vec, vec, vec,
        ],
        out_specs=pl.BlockSpec((1, tt, w), lambda bi, ti: (bi, ti, 0)),
        out_shape=jax.ShapeDtypeStruct((b, s, w), BF16),
        scratch_shapes=[pltpu.VMEM((tt + 8, w), F32), pltpu.VMEM((1, w), F32)],
        compiler_params=_params(("parallel", "arbitrary")),
        name="rglru",
    )(pf3, pf3, conv_w, conv_b, w_a, b_a, w_x, b_x, lam, gain)


def _mlstm_body(q_ref, k_ref, v_ref, og_ref, gc_ref, gr_ref, bc_ref, br_ref, hg_ref,
                out_ref, c_ref, m_ref):
    c = pl.program_id(1)
    ln = q_ref.shape[1]

    @pl.when(c == 0)
    def _():
        c_ref[...] = jnp.zeros_like(c_ref)
        m_ref[...] = jnp.zeros_like(m_ref)

    gcol = gc_ref[0] + bc_ref[...]
    grow = gr_ref[...] + br_ref[...]
    lf_col = _log_sigmoid(gcol)
    lf_row = _log_sigmoid(grow)
    ri = lax.broadcasted_iota(jnp.int32, (ln, ln), 0)
    ci = lax.broadcasted_iota(jnp.int32, (ln, ln), 1)
    causal = ri >= ci
    tri_l = jnp.where(causal, 1.0, 0.0).astype(BF16)
    tri_u = jnp.where(ri <= ci, 1.0, 0.0).astype(BF16)
    b_col = sum(_dot(tri_l, part) for part in _split_bf16(lf_col, 3))
    b_row = sum(_dot(part, tri_u) for part in _split_bf16(lf_row, 3))

    ones_blk = jnp.ones((ln, HEAD_DIM), BF16)
    reps = ln // HEAD_DIM

    def wide(x):
        return jnp.concatenate([x] * reps, axis=1)

    heads = [slice(h * HEAD_DIM, (h + 1) * HEAD_DIM) for h in range(N_HEADS)]
    qs = [q_ref[0, :, sl] for sl in heads]
    ks = [k_ref[0, :, sl] for sl in heads]
    v_augs = [jnp.concatenate([v_ref[0, :, sl], ones_blk], axis=1) for sl in heads]
    c_augs = [c_ref[h] for h in range(N_HEADS)]

    decay_w, m_ts, w_inters, m_nexts, decays, kws = [], [], [], [], [], []
    for h in range(N_HEADS):
        bc = jnp.broadcast_to(b_col[:, N_HEADS + h:N_HEADS + h + 1], (ln, HEAD_DIM))
        ig = jnp.broadcast_to(gcol[:, h:h + 1], (ln, HEAD_DIM))
        brow = b_row[N_HEADS + h:N_HEADS + h + 1, :]
        igr = grow[h:h + 1, :]
        m_run = m_ref[h:h + 1, :]
        b_last = bc[ln - 1:ln, :]
        log_d = jnp.where(causal, wide(bc) - brow + igr, NEG_BIG)
        inter = bc + m_run
        m_t = jnp.maximum(inter, jnp.max(log_d, axis=1, keepdims=True))
        decay_w.append(jnp.exp(log_d - wide(m_t)))
        m_ts.append(m_t)
        w_inters.append(jnp.exp(inter - m_t))
        log_w = b_last - bc + ig
        m_next = jnp.maximum(b_last + m_run, jnp.max(log_w, axis=0, keepdims=True))
        m_nexts.append(m_next)
        decays.append(jnp.exp(b_last + m_run - m_next))
        kws.append((ks[h].astype(F32) * jnp.exp(log_w - m_next)).astype(BF16))

    qk = [_dot_nt(q, k) for q, k in zip(qs, ks)]
    qc = [_dot(q, c_aug.astype(BF16)) for q, c_aug in zip(qs, c_augs)]
    smats = [(s * d).astype(BF16) for s, d in zip(qk, decay_w)]
    intra = [_dot(s, v_aug) for s, v_aug in zip(smats, v_augs)]
    upd = [lax.dot_general(kw, v_aug, _TN, preferred_element_type=F32) for kw, v_aug in zip(kws, v_augs)]

    for h, sl in enumerate(heads):
        num = intra[h][:, :HEAD_DIM] + w_inters[h] * qc[h][:, :HEAD_DIM]
        den = intra[h][:, HEAD_DIM:] + w_inters[h] * qc[h][:, HEAD_DIM:]
        hh = num / jnp.maximum(jnp.abs(den), jnp.exp(-m_ts[h]))
        c_ref[h] = jnp.concatenate([decays[h]] * 2, axis=1) * c_augs[h] + upd[h]
        m_ref[h:h + 1, :] = m_nexts[h]
        hn = _rms(hh, hg_ref[:, sl])
        out_ref[0, :, sl] = (hn * jax.nn.sigmoid(og_ref[0, :, sl])).astype(BF16)


def _mlstm(pf3, pb3, gates_c, gates_r, bias_c, bias_r, head_gain, ln=256):
    b, s, _ = pf3.shape
    nc = s // ln
    w = GROUP_WIDTH

    def col(g):
        return pl.BlockSpec((1, ln, w), lambda bi, ci, g=g: (bi, ci, g))

    return pl.pallas_call(
        _mlstm_body,
        grid=(b, nc),
        in_specs=[
            col(PB_MQ), col(PB_MK), col(PB_MV), col(PF_MO),
            pl.BlockSpec((1, ln, GATE_LANES), lambda bi, ci: (bi, ci, 0)),
            pl.BlockSpec((GATE_ROWS, ln), lambda bi, ci: (0, bi * nc + ci)),
            pl.BlockSpec((1, GATE_LANES), lambda bi, ci: (0, 0)),
            pl.BlockSpec((GATE_ROWS, 1), lambda bi, ci: (0, 0)),
            pl.BlockSpec((1, w), lambda bi, ci: (0, 0)),
        ],
        out_specs=pl.BlockSpec((1, ln, w), lambda bi, ci: (bi, ci, 0)),
        out_shape=jax.ShapeDtypeStruct((b, s, w), BF16),
        scratch_shapes=[pltpu.VMEM((N_HEADS, HEAD_DIM, 2 * HEAD_DIM), F32),
                        pltpu.VMEM((8, 128), F32)],
        compiler_params=_params(("parallel", "arbitrary")),
        name="mlstm",
    )(pb3, pb3, pb3, pf3, gates_c.reshape(b, s, GATE_LANES), gates_r, bias_c, bias_r, head_gain)


def _dil_body(q_ref, k_ref, v_ref, qg_ref, kg_ref, sl_ref, o_ref, qn, kn, m_s, l_s):
    s = q_ref.shape[1]
    n_blocks = s // BLK
    vv, o2 = v_ref.at[0], o_ref.at[0]
    qn[...] = _rms(q_ref[0], qg_ref[...]) * ATTN_SCALE
    kn[...] = _rms(k_ref[0], kg_ref[...])
    slope = sl_ref[0, 0:1, 0:1]
    qq = lax.broadcasted_iota(jnp.int32, (BLK, 2 * BLK), 0)
    kk = lax.broadcasted_iota(jnp.int32, (BLK, 2 * BLK), 1)
    dist = jnp.where(kk < BLK, qq - kk, qq - kk + 2 * BLK)
    in_window = jnp.logical_and(dist >= 0, dist <= BLK)
    prev_lanes = lax.broadcasted_iota(jnp.int32, (1, 2 * BLK), 1) >= BLK
    ones = jnp.ones((2 * BLK, HEAD_DIM), BF16)
    assert n_blocks % DIL_PAR == 0

    for pi, (window, dil) in enumerate(reversed(DILATED_PATTERNS)):
        assert window // dil == BLK and s % (dil * BLK) == 0
        nb = s // (dil * BLK)
        assert nb % DIL_PAR == 0 or DIL_PAR % nb == 0
        bias = jnp.where(in_window, (-float(dil) * slope) * dist.astype(F32), NEG_BIG)
        first, last = pi == 0, pi == len(DILATED_PATTERNS) - 1

        def rows(start, dil=dil):
            return pl.ds(start, BLK) if dil == 1 else pl.ds(start, BLK, stride=dil)

        def group(t0, carry, dil=dil, nb=nb, bias=bias, first=first, last=last, rows=rows):
            cur, qb, kc, vc, no_prev = [], [], [], [], []
            for i in range(DIL_PAR):
                t = t0 * DIL_PAR + i
                r = t // nb
                n = t - r * nb
                c = rows(r + dil * BLK * n)
                cur.append(c)
                qb.append(qn[c, :].astype(BF16))
                kc.append(kn[c, :].astype(BF16))
                vc.append(vv[c, :].astype(BF16))
                no_prev.append(jnp.where(prev_lanes, jnp.where(n > 0, 0.0, NEG_BIG), 0.0))
                if i == 0:
                    p = rows(r + dil * BLK * jnp.maximum(n - 1, 0))
                    kp, vp = [kn[p, :].astype(BF16)], [vv[p, :].astype(BF16)]
                elif i % nb == 0:
                    kp.append(kc[i])
                    vp.append(vc[i])
                else:
                    kp.append(kc[i - 1])
                    vp.append(vc[i - 1])
            k2 = [jnp.concatenate([a, b], axis=0) for a, b in zip(kc, kp)]
            v2 = [jnp.concatenate([a, b], axis=0) for a, b in zip(vc, vp)]
            sc = [jnp.maximum(_dot_nt(q, k) + bias + off, NEG_BIG) for q, k, off in zip(qb, k2, no_prev)]
            m_b = [jnp.max(x, axis=1, keepdims=True) for x in sc]
            pr = [jnp.exp(x - m).astype(BF16) for x, m in zip(sc, m_b)]
            nd = [_dot(p, jnp.concatenate([v, ones], axis=1)) for p, v in zip(pr, v2)]
            res = []
            for c, m, x in zip(cur, m_b, nd):
                num, den = x[:, :HEAD_DIM], x[:, HEAD_DIM:]
                if first:
                    res.append((jnp.broadcast_to(m, (BLK, HEAD_DIM)), den, num))
                else:
                    m_o = m_s[c, :]
                    m_n = jnp.maximum(m_o, m)
                    a_o = jnp.exp(m_o - m_n)
                    a_b = jnp.exp(m - m_n)
                    res.append((m_n, l_s[c, :] * a_o + den * a_b, o2[c, :] * a_o + num * a_b))
            for c, (m_n, l_n, acc) in zip(cur, res):
                if last:
                    o2[c, :] = acc / l_n
                else:
                    m_s[c, :] = m_n
                    l_s[c, :] = l_n
                    o2[c, :] = acc
            return carry

        lax.fori_loop(0, n_blocks // DIL_PAR, group, 0)


def _dilated(pf3, q_gain, k_gain, slopes):
    b, s, _ = pf3.shape

    def col(g):
        return pl.BlockSpec((1, s, HEAD_DIM), lambda bi, hi, g=g: (bi, 0, g * N_HEADS + hi))

    vec = pl.BlockSpec((1, HEAD_DIM), lambda bi, hi: (0, 0))
    return pl.pallas_call(
        _dil_body,
        grid=(b, N_HEADS),
        in_specs=[col(PF_CQ), col(PF_CK), col(PF_CV), vec, vec,
                  pl.BlockSpec((1, 8, HEAD_DIM), lambda bi, hi: (hi, 0, 0))],
        out_specs=pl.BlockSpec((1, s, HEAD_DIM), lambda bi, hi: (bi, 0, hi)),
        out_shape=jax.ShapeDtypeStruct((b, s, GROUP_WIDTH), F32),
        scratch_shapes=[pltpu.VMEM((s, HEAD_DIM), F32)] * 4,
        compiler_params=_params(("parallel", "parallel")),
        name="dilated",
    )(pf3, pf3, pf3, q_gain, k_gain, slopes)


def _sb_body(q_ref, k_ref, v_ref, *rest):
    n_cast = (len(rest) - 2) // 2
    cast_src, o_ref, cast_dst, z_ref = rest[:n_cast], rest[n_cast], rest[n_cast + 1:-1], rest[-1]
    for src, dst in zip(cast_src, cast_dst):
        dst[...] = src[...].astype(BF16)
    i = pl.program_id(1)
    nq = SB_BLK
    qq = lax.broadcasted_iota(jnp.int32, (nq, nq), 0)
    kk = lax.broadcasted_iota(jnp.int32, (nq, nq), 1)
    strict = kk < qq
    after = jnp.where(qq > kk, 1.0, 0.0).astype(BF16)
    after2 = jnp.concatenate([after, after], axis=0)
    o_ref[...] = jnp.zeros_like(o_ref)

    heads = [slice(h * HEAD_DIM, (h + 1) * HEAD_DIM) for h in range(N_HEADS)]

    def logits(j):
        keys = pl.ds(pl.multiple_of(j * nq, nq), nq)
        return [_dot_nt(q_ref[0, :, sl], k_ref[0, keys, sl]) for sl in heads]

    def step(j, gone, masked):
        keys = pl.ds(pl.multiple_of(j * nq, nq), nq)
        zs = [z_ref[h] for h in range(N_HEADS)]
        sps = [jnp.maximum(z, 0.0) + jnp.log2(1.0 + jnp.exp2(-jnp.abs(z))) for z in zs]
        log_beta = [z - sp for z, sp in zip(zs, sps)]
        drops = [jnp.where(strict, sp, 0.0) for sp in sps] if masked else sps
        gone_next = tuple(g + jnp.sum(drop, axis=1, keepdims=True) for g, drop in zip(gone, drops))
        for h, z in enumerate(logits(jnp.maximum(j - 1, 0))):
            z_ref[h] = z
        parts = [jnp.concatenate(_split_bf16(drop, 2), axis=1) for drop in drops]
        laters = [_dot(part, after2) for part in parts]
        ws = [jnp.exp2(lb - later - g) for lb, later, g in zip(log_beta, laters, gone)]
        if masked:
            ws = [jnp.where(strict, w, 0.0) for w in ws]
        pv = [_dot(w.astype(BF16), v_ref[0, keys, sl]) for w, sl in zip(ws, heads)]
        o_ref[0] += jnp.concatenate(pv, axis=1)
        return gone_next

    for h, z in enumerate(logits(i)):
        z_ref[h] = z
    gone = step(i, tuple(jnp.zeros((nq, 1), F32) for _ in range(N_HEADS)), True)
    lax.fori_loop(0, i, lambda jj, g: step(i - 1 - jj, g, False), gone)


def _cast_rows(rows, steps):
    rb = -(-rows // steps)
    rb = -(-rb // BF16_ROWS) * BF16_ROWS
    while rows % rb:
        rb += BF16_ROWS
    return rb


def _stick_breaking(pb3, cast_jobs=()):
    b, s, _ = pb3.shape
    w = GROUP_WIDTH
    nq = s // SB_BLK
    cast_in, cast_out, cast_shapes = [], [], []
    for arr, layer in cast_jobs:
        _, rows, cols = arr.shape
        rb = _cast_rows(rows, b * nq)
        step = lambda bi, qi, last=rows // rb - 1: jnp.minimum(bi * nq + qi, last)
        cast_in.append(pl.BlockSpec((None, rb, cols), lambda bi, qi, f=step, l=layer: (l, f(bi, qi), 0)))
        cast_out.append(pl.BlockSpec((rb, cols), lambda bi, qi, f=step: (f(bi, qi), 0)))
        cast_shapes.append(jax.ShapeDtypeStruct((rows, cols), BF16))
    out = pl.pallas_call(
        _sb_body,
        grid=(b, nq),
        in_specs=[
            pl.BlockSpec((1, SB_BLK, w), lambda bi, qi: (bi, qi, PB_SQ)),
            pl.BlockSpec((1, s, w), lambda bi, qi: (bi, 0, PB_SK)),
            pl.BlockSpec((1, s, w), lambda bi, qi: (bi, 0, PB_SV)),
        ] + cast_in,
        out_specs=[pl.BlockSpec((1, SB_BLK, w), lambda bi, qi: (bi, qi, 0))] + cast_out,
        out_shape=[jax.ShapeDtypeStruct((b, s, w), F32)] + cast_shapes,
        scratch_shapes=[pltpu.VMEM((N_HEADS, SB_BLK, SB_BLK), F32)],
        compiler_params=_params(("arbitrary", "arbitrary")),
        name="stick_breaking",
    )(pb3, pb3, pb3, *[arr for arr, _ in cast_jobs])
    return out[0], tuple(out[1:])


def _cast_body(src, dst):
    dst[...] = src[...].astype(BF16)


def _cast_layer(stacked, layer):
    _, rows, cols = stacked.shape
    rb = _cast_rows(rows, CAST_STEPS)
    return pl.pallas_call(
        _cast_body,
        grid=(rows // rb,),
        in_specs=[pl.BlockSpec((None, rb, cols), lambda i: (layer, i, 0))],
        out_specs=pl.BlockSpec((rb, cols), lambda i: (i, 0)),
        out_shape=jax.ShapeDtypeStruct((rows, cols), BF16),
        compiler_params=_params(("parallel",)),
        name="cast_bf16",
    )(stacked)


def _outproj_body(x_ref, ya_ref, yb_ref, yc_ref, yd_ref, gc_ref, gd_ref, w_ref, o_ref):
    yc = _rms(yc_ref[...], gc_ref[...]).astype(BF16)
    yd = _rms(yd_ref[...], gd_ref[...]).astype(BF16)
    w = GROUP_WIDTH
    acc = _dot(ya_ref[...], w_ref[0:w, :])
    acc += _dot(yb_ref[...], w_ref[w:2 * w, :])
    acc += _dot(yc, w_ref[2 * w:3 * w, :])
    acc += _dot(yd, w_ref[3 * w:4 * w, :])
    o_ref[...] = x_ref[...] + acc


def _outproj(x, ya, yb, yc, yd, gain_c, gain_d, w_out, layer, tm=512):
    m, d = x.shape
    w = GROUP_WIDTH
    yblk = pl.BlockSpec((tm, w), lambda i: (i, 0))
    vec = pl.BlockSpec((1, w), lambda i: (0, 0))
    return pl.pallas_call(
        _outproj_body,
        grid=(m // tm,),
        in_specs=[pl.BlockSpec((tm, d), lambda i: (i, 0)), yblk, yblk, yblk, yblk, vec, vec,
                  pl.BlockSpec((None, 4 * w, d), lambda i: (layer, 0, 0))],
        out_specs=pl.BlockSpec((tm, d), lambda i: (i, 0)),
        out_shape=jax.ShapeDtypeStruct((m, d), F32),
        compiler_params=_params(("parallel",)),
        name="outproj",
    )(x, ya, yb, yc, yd, gain_c, gain_d, w_out)


def _row(v):
    return v.reshape(1, -1).astype(F32)


def _split_w_in(w_in):
    w = GROUP_WIDTH
    gate_lo = 6 * w
    gate_hi = gate_lo + 2 * N_HEADS
    w_f = jnp.concatenate([w_in[..., 0:2 * w], w_in[..., 5 * w:6 * w], w_in[..., gate_hi:gate_hi + 3 * w]], axis=-1)
    w_b = jnp.concatenate([w_in[..., 2 * w:5 * w], w_in[..., gate_hi + 3 * w:]], axis=-1)
    w_gate = jnp.pad(w_in[..., gate_lo:gate_hi], ((0, 0), (0, 0), (0, GATE_LANES - 2 * N_HEADS)))
    return w_f.astype(BF16), w_b.astype(BF16), w_gate.astype(BF16)


def kernel(x, ffn1_norm, ffn1_w_gate, ffn1_w_up, ffn1_w_down, mix_norm, w_in, lru_conv_w, lru_conv_b, lru_w_a, lru_b_a, lru_w_x, lru_b_x, lru_lambda, mlstm_ig_bias, mlstm_fg_bias, attn_q_gain, attn_k_gain, group_out_gain, w_out, ffn2_norm, ffn2_w_gate, ffn2_w_up, ffn2_w_down):
    b, s, d = x.shape
    depth = w_in.shape[0]
    m = b * s
    w = GROUP_WIDTH
    slopes = 2.0 ** (-8.0 * jnp.arange(1, N_HEADS + 1, dtype=F32) / N_HEADS)
    slopes = jnp.broadcast_to(slopes[:, None, None], (N_HEADS, 8, HEAD_DIM))
    col_scale = jnp.ones((6, w), F32).at[PB_MK].set(ATTN_SCALE).at[PB_SQ].set(SB_Q_SCALE).reshape(1, D_HALF)

    ffn_w = tuple(_cast_layer(t, 0) for t in (ffn1_w_gate, ffn1_w_up, ffn1_w_down))
    w_f, w_b, w_gate = _split_w_in(w_in)
    w_out_b = w_out.astype(BF16)

    xf = x.reshape(m, d)
    for l in range(depth):
        xf = _ffn(xf, _row(ffn1_norm[l]), ffn_w)

        pf, pb, gates_c, gates_r = _inproj(xf, _row(mix_norm[l]), w_f, w_b, col_scale, w_gate, l)
        pf3 = pf.reshape(b, s, D_HALF)
        pb3 = pb.reshape(b, s, D_HALF)

        gains = group_out_gain[l].reshape(4, 1, w)
        ya = _lru(pf3, lru_conv_w[l], _row(lru_conv_b[l]), lru_w_a[l].astype(BF16), _row(lru_b_a[l]),
                  lru_w_x[l].astype(BF16), _row(lru_b_x[l]), _row(lru_lambda[l]), gains[0])

        gate_bias = jnp.concatenate([mlstm_ig_bias[l], mlstm_fg_bias[l]]).astype(F32)
        bias_c = jnp.pad(gate_bias, (0, GATE_LANES - 2 * N_HEADS)).reshape(1, GATE_LANES)
        bias_r = jnp.pad(gate_bias, (0, GATE_ROWS - 2 * N_HEADS)).reshape(GATE_ROWS, 1)
        yb = _mlstm(pf3, pb3, gates_c, gates_r, bias_c, bias_r, gains[1])

        yc = _dilated(pf3, _row(attn_q_gain[l]), _row(attn_k_gain[l]), slopes)
        jobs = [(t, l) for t in (ffn2_w_gate, ffn2_w_up, ffn2_w_down)]
        if l + 1 < depth:
            jobs += [(t, l + 1) for t in (ffn1_w_gate, ffn1_w_up, ffn1_w_down)]
        yd, cast = _stick_breaking(pb3, jobs)
        ffn2_w, ffn_w = cast[:3], cast[3:]

        xf = _outproj(xf, ya.reshape(m, w), yb.reshape(m, w), yc.reshape(m, w), yd.reshape(m, w),
                      gains[2], gains[3], w_out_b, l)

        xf = _ffn(xf, _row(ffn2_norm[l]), ffn2_w)
    return xf.reshape(b, s, d)
```

```python
import jax
import jax.numpy as jnp
from jax import lax
from jax.experimental import pallas as pl
from jax.experimental.pallas import tpu as pltpu

F32 = jnp.float32
BF16 = jnp.bfloat16

D_MODEL = 2048
N_HEADS = 4
HEAD_DIM = 128
GROUP_WIDTH = 512
D_FF = 5504
FF_BLK = 128
FF_SUB = 4
FF_TILE = FF_SUB * FF_BLK
LRU_C = 8.0
LRU_CONV_WIDTH = 4
DILATED_PATTERNS = ((128, 1), (512, 4), (2048, 16))
BLK = 128
SB_BLK = 256
DIL_PAR = 8
RMS_EPS = 1e-6
NEG_BIG = -1e30
ATTN_SCALE = HEAD_DIM ** -0.5
SB_Q_SCALE = ATTN_SCALE * 1.4426950408889634
GATE_LANES = 128
GATE_ROWS = 16
SUBLANES = 8
BF16_ROWS = 16
CAST_STEPS = 8
VMEM_LIMIT = 52 * 1024 * 1024
FFN_VMEM_LIMIT = 58 * 1024 * 1024

PF_LX, PF_LG, PF_MO, PF_CQ, PF_CK, PF_CV = range(6)
PB_MQ, PB_MK, PB_MV, PB_SQ, PB_SK, PB_SV = range(6)
D_HALF = 6 * GROUP_WIDTH
PROJ_TILE = 512

_NT = (((1,), (1,)), ((), ()))
_TN = (((0,), (0,)), ((), ()))


def _rms(x, gain):
    return x * lax.rsqrt(jnp.mean(x * x, axis=-1, keepdims=True) + RMS_EPS) * gain


def _softplus(x):
    return jnp.maximum(x, 0.0) + jnp.log(1.0 + jnp.exp(-jnp.abs(x)))


def _log_sigmoid(x):
    return -_softplus(-x)


def _dot(a, b):
    return jnp.dot(a, b, preferred_element_type=F32)


def _dot_nt(a, b):
    return lax.dot_general(a, b, _NT, preferred_element_type=F32)


def _split_bf16(x, parts):
    out = []
    r = x
    for _ in range(parts):
        t = r.astype(BF16)
        out.append(t)
        r = r - t.astype(F32)
    return out


def _params(sem, vmem_limit=None):
    return pltpu.CompilerParams(dimension_semantics=sem, vmem_limit_bytes=vmem_limit or VMEM_LIMIT)


def _ffn_body(x_ref, g_ref, *refs):
    wg_refs, wu_refs, wd_refs = refs[:FF_SUB], refs[FF_SUB:2 * FF_SUB], refs[2 * FF_SUB:3 * FF_SUB]
    o_ref, h_ref = refs[3 * FF_SUB:]
    j = pl.program_id(1)

    @pl.when(j == 0)
    def _():
        h_ref[...] = _rms(x_ref[...], g_ref[...]).astype(BF16)
        o_ref[...] = jnp.zeros_like(o_ref)

    h = h_ref[...]
    g = _dot(h, jnp.concatenate([r[...] for r in wg_refs], axis=1))
    u = _dot(h, jnp.concatenate([r[...] for r in wu_refs], axis=1))
    col = lax.broadcasted_iota(jnp.int32, (1, FF_TILE), 1)
    a = jnp.where(col < D_FF - j * FF_TILE, (g * jax.nn.sigmoid(g)) * u, 0.0)
    o_ref[...] += _dot(a.astype(BF16), jnp.concatenate([r[...] for r in wd_refs], axis=0))

    @pl.when(j == pl.num_programs(1) - 1)
    def _():
        o_ref[...] = x_ref[...] + 0.5 * o_ref[...]


def _ffn(x, gain, weights, tm=1024):
    wg, wu, wd = weights
    m, d = x.shape
    last_blk = D_FF // FF_BLK - 1

    def blk(k):
        return lambda i, j: jnp.minimum(j * FF_SUB + k, last_blk)

    cols = [pl.BlockSpec((d, FF_BLK), lambda i, j, f=blk(k): (0, f(i, j))) for k in range(FF_SUB)]
    rows = [pl.BlockSpec((FF_BLK, d), lambda i, j, f=blk(k): (f(i, j), 0)) for k in range(FF_SUB)]
    return pl.pallas_call(
        _ffn_body,
        grid=(m // tm, pl.cdiv(D_FF, FF_TILE)),
        in_specs=[pl.BlockSpec((tm, d), lambda i, j: (i, 0)),
                  pl.BlockSpec((1, d), lambda i, j: (0, 0))] + cols + cols + rows,
        out_specs=pl.BlockSpec((tm, d), lambda i, j: (i, 0)),
        out_shape=jax.ShapeDtypeStruct((m, d), F32),
        scratch_shapes=[pltpu.VMEM((tm, d), BF16)],
        compiler_params=_params(("parallel", "arbitrary"), FFN_VMEM_LIMIT),
        name="ffn",
    )(x, gain, *([wg] * FF_SUB), *([wu] * FF_SUB), *([wd] * FF_SUB))


def _inproj_body(x_ref, g_ref, wf_ref, wb_ref, sc_ref, wgc_ref,
                 pf_ref, pb_ref, gc_ref, gr_ref, h_ref):
    j = pl.program_id(1)

    @pl.when(j == 0)
    def _():
        h = _rms(x_ref[...], g_ref[...]).astype(BF16)
        h_ref[...] = h
        gates = _dot(h, wgc_ref[...])
        gc_ref[...] = gates
        gr_ref[...] = gates.T[:GATE_ROWS, :]

    h = h_ref[...]
    pf_ref[...] = _dot(h, wf_ref[...])
    pb_ref[...] = (_dot(h, wb_ref[...]) * sc_ref[...]).astype(BF16)


def _inproj(x, gain, w_f, w_b, col_scale, w_gate_c, tm=1024):
    m, d = x.shape
    tn = PROJ_TILE
    return pl.pallas_call(
        _inproj_body,
        grid=(m // tm, D_HALF // tn),
        in_specs=[
            pl.BlockSpec((tm, d), lambda i, j: (i, 0)),
            pl.BlockSpec((1, d), lambda i, j: (0, 0)),
            pl.BlockSpec((d, tn), lambda i, j: (0, j)),
            pl.BlockSpec((d, tn), lambda i, j: (0, j)),
            pl.BlockSpec((1, tn), lambda i, j: (0, j)),
            pl.BlockSpec((d, GATE_LANES), lambda i, j: (0, 0)),
        ],
        out_specs=[
            pl.BlockSpec((tm, tn), lambda i, j: (i, j)),
            pl.BlockSpec((tm, tn), lambda i, j: (i, j)),
            pl.BlockSpec((tm, GATE_LANES), lambda i, j: (i, 0)),
            pl.BlockSpec((GATE_ROWS, tm), lambda i, j: (0, i)),
        ],
        out_shape=[
            jax.ShapeDtypeStruct((m, D_HALF), F32),
            jax.ShapeDtypeStruct((m, D_HALF), BF16),
            jax.ShapeDtypeStruct((m, GATE_LANES), F32),
            jax.ShapeDtypeStruct((GATE_ROWS, m), F32),
        ],
        scratch_shapes=[pltpu.VMEM((tm, d), BF16)],
        compiler_params=_params(("parallel", "arbitrary")),
        name="inproj",
    )(x, gain, w_f, w_b, col_scale, w_gate_c)


def _lru_body(xr_ref, gate_ref, cw_ref, cb_ref, wa_ref, ba_ref, wx_ref, bx_ref, lam_ref,
              gain_ref, o_ref, xbuf, hcar):
    t = pl.program_id(1)
    tt = xr_ref.shape[1]
    width = xr_ref.shape[2]
    pad = 8

    @pl.when(t == 0)
    def _():
        xbuf[0:pad, :] = jnp.zeros((pad, width), F32)
        hcar[...] = jnp.zeros_like(hcar)

    xr = xr_ref[0]
    xbuf[pad:pad + tt, :] = xr
    xc = cb_ref[...]
    for j in range(LRU_CONV_WIDTH):
        off = pad - (LRU_CONV_WIDTH - 1) + j
        xc = xc + cw_ref[j:j + 1, :] * xbuf[pl.ds(off, tt), :]
    xbuf[0:pad, :] = xr[tt - pad:tt, :]

    xcb = xc.astype(BF16)
    ra, rx = [], []
    for n in range(N_HEADS):
        blk = xcb[:, n * HEAD_DIM:(n + 1) * HEAD_DIM]
        ra.append(_dot(blk, wa_ref[n]))
        rx.append(_dot(blk, wx_ref[n]))
    r = jax.nn.sigmoid(jnp.concatenate(ra, axis=1) + ba_ref[...])
    i = jax.nn.sigmoid(jnp.concatenate(rx, axis=1) + bx_ref[...])
    log_a = (-LRU_C * _softplus(-lam_ref[...])) * r
    a = jnp.exp(log_a)
    u = jnp.sqrt(-jnp.tanh(log_a) * (a * a + 1.0)) * (i * xc)

    sub = lax.broadcasted_iota(jnp.int32, (tt, width), 0) % SUBLANES
    s = 1
    while s < SUBLANES:
        keep = sub >= s
        a_sh = jnp.where(keep, pltpu.roll(a, s, 0), 1.0)
        u_sh = jnp.where(keep, pltpu.roll(u, s, 0), 0.0)
        u = a * u_sh + u
        a = a * a_sh
        s *= 2
    carry = hcar[...]
    groups = []
    for g in range(tt // SUBLANES):
        rows = slice(g * SUBLANES, (g + 1) * SUBLANES)
        hg = u[rows, :] + a[rows, :] * carry
        groups.append(hg)
        carry = hg[SUBLANES - 1:SUBLANES, :]
    h = jnp.concatenate(groups, axis=0)
    hcar[...] = carry

    y = h * jax.nn.gelu(gate_ref[0])
    o_ref[0] = _rms(y, gain_ref[...]).astype(BF16)


def _lru(pf3, conv_w, conv_b, w_a, b_a, w_x, b_x, lam, gain, tt=512):
    b, s, _ = pf3.shape
    w = GROUP_WIDTH
    vec = pl.BlockSpec((1, w), lambda bi, ti: (0, 0))
    mat = pl.BlockSpec((N_HEADS, HEAD_DIM, HEAD_DIM), lambda bi, ti: (0, 0, 0))
    return pl.pallas_call(
        _lru_body,
        grid=(b, s // tt),
        in_specs=[
            pl.BlockSpec((1, tt, w), lambda bi, ti: (bi, ti, PF_LX)),
            pl.BlockSpec((1, tt, w), lambda bi, ti: (bi, ti, PF_LG)),
            pl.BlockSpec((LRU_CONV_WIDTH, w), lambda bi, ti: (0, 0)),
            vec, mat, vec, mat, vec, vec, vec,
        ],
        out_specs=pl.BlockSpec((1, tt, w), lambda bi, ti: (bi, ti, 0)),
        out_shape=jax.ShapeDtypeStruct((b, s, w), BF16),
        scratch_shapes=[pltpu.VMEM((tt + 8, w), F32), pltpu.VMEM((1, w), F32)],
        compiler_params=_params(("parallel", "arbitrary")),
        name="rglru",
    )(pf3, pf3, conv_w, conv_b, w_a, b_a, w_x, b_x, lam, gain)


def _mlstm_body(q_ref, k_ref, v_ref, og_ref, gc_ref, gr_ref, bc_ref, br_ref, hg_ref,
                out_ref, c_ref, m_ref):
    c = pl.program_id(1)
    ln = q_ref.shape[1]

    @pl.when(c == 0)
    def _():
        c_ref[...] = jnp.zeros_like(c_ref)
        m_ref[...] = jnp.zeros_like(m_ref)

    gcol = gc_ref[0] + bc_ref[...]
    grow = gr_ref[...] + br_ref[...]
    lf_col = _log_sigmoid(gcol)
    lf_row = _log_sigmoid(grow)
    ri = lax.broadcasted_iota(jnp.int32, (ln, ln), 0)
    ci = lax.broadcasted_iota(jnp.int32, (ln, ln), 1)
    causal = ri >= ci
    tri_l = jnp.where(causal, 1.0, 0.0).astype(BF16)
    tri_u = jnp.where(ri <= ci, 1.0, 0.0).astype(BF16)
    b_col = sum(_dot(tri_l, part) for part in _split_bf16(lf_col, 3))
    b_row = sum(_dot(part, tri_u) for part in _split_bf16(lf_row, 3))

    ones_blk = jnp.ones((ln, HEAD_DIM), BF16)
    reps = ln // HEAD_DIM

    def wide(x):
        return jnp.concatenate([x] * reps, axis=1)

    heads = [slice(h * HEAD_DIM, (h + 1) * HEAD_DIM) for h in range(N_HEADS)]
    qs = [q_ref[0, :, sl] for sl in heads]
    ks = [k_ref[0, :, sl] for sl in heads]
    v_augs = [jnp.concatenate([v_ref[0, :, sl], ones_blk], axis=1) for sl in heads]
    c_augs = [c_ref[h] for h in range(N_HEADS)]

    decay_w, m_ts, w_inters, m_nexts, decays, kws = [], [], [], [], [], []
    for h in range(N_HEADS):
        bc = jnp.broadcast_to(b_col[:, N_HEADS + h:N_HEADS + h + 1], (ln, HEAD_DIM))
        ig = jnp.broadcast_to(gcol[:, h:h + 1], (ln, HEAD_DIM))
        brow = b_row[N_HEADS + h:N_HEADS + h + 1, :]
        igr = grow[h:h + 1, :]
        m_run = m_ref[h:h + 1, :]
        b_last = bc[ln - 1:ln, :]
        log_d = jnp.where(causal, wide(bc) - brow + igr, NEG_BIG)
        inter = bc + m_run
        m_t = jnp.maximum(inter, jnp.max(log_d, axis=1, keepdims=True))
        decay_w.append(jnp.exp(log_d - wide(m_t)))
        m_ts.append(m_t)
        w_inters.append(jnp.exp(inter - m_t))
        log_w = b_last - bc + ig
        m_next = jnp.maximum(b_last + m_run, jnp.max(log_w, axis=0, keepdims=True))
        m_nexts.append(m_next)
        decays.append(jnp.exp(b_last + m_run - m_next))
        kws.append((ks[h].astype(F32) * jnp.exp(log_w - m_next)).astype(BF16))

    qk = [_dot_nt(q, k) for q, k in zip(qs, ks)]
    qc = [_dot(q, c_aug.astype(BF16)) for q, c_aug in zip(qs, c_augs)]
    smats = [(s * d).astype(BF16) for s, d in zip(qk, decay_w)]
    intra = [_dot(s, v_aug) for s, v_aug in zip(smats, v_augs)]
    upd = [lax.dot_general(kw, v_aug, _TN, preferred_element_type=F32) for kw, v_aug in zip(kws, v_augs)]

    for h, sl in enumerate(heads):
        num = intra[h][:, :HEAD_DIM] + w_inters[h] * qc[h][:, :HEAD_DIM]
        den = intra[h][:, HEAD_DIM:] + w_inters[h] * qc[h][:, HEAD_DIM:]
        hh = num / jnp.maximum(jnp.abs(den), jnp.exp(-m_ts[h]))
        c_ref[h] = jnp.concatenate([decays[h]] * 2, axis=1) * c_augs[h] + upd[h]
        m_ref[h:h + 1, :] = m_nexts[h]
        hn = _rms(hh, hg_ref[:, sl])
        out_ref[0, :, sl] = (hn * jax.nn.sigmoid(og_ref[0, :, sl])).astype(BF16)


def _mlstm(pf3, pb3, gates_c, gates_r, bias_c, bias_r, head_gain, ln=256):
    b, s, _ = pf3.shape
    nc = s // ln
    w = GROUP_WIDTH

    def col(g):
        return pl.BlockSpec((1, ln, w), lambda bi, ci, g=g: (bi, ci, g))

    return pl.pallas_call(
        _mlstm_body,
        grid=(b, nc),
        in_specs=[
            col(PB_MQ), col(PB_MK), col(PB_MV), col(PF_MO),
            pl.BlockSpec((1, ln, GATE_LANES), lambda bi, ci: (bi, ci, 0)),
            pl.BlockSpec((GATE_ROWS, ln), lambda bi, ci: (0, bi * nc + ci)),
            pl.BlockSpec((1, GATE_LANES), lambda bi, ci: (0, 0)),
            pl.BlockSpec((GATE_ROWS, 1), lambda bi, ci: (0, 0)),
            pl.BlockSpec((1, w), lambda bi, ci: (0, 0)),
        ],
        out_specs=pl.BlockSpec((1, ln, w), lambda bi, ci: (bi, ci, 0)),
        out_shape=jax.ShapeDtypeStruct((b, s, w), BF16),
        scratch_shapes=[pltpu.VMEM((N_HEADS, HEAD_DIM, 2 * HEAD_DIM), F32),
                        pltpu.VMEM((8, 128), F32)],
        compiler_params=_params(("parallel", "arbitrary")),
        name="mlstm",
    )(pb3, pb3, pb3, pf3, gates_c.reshape(b, s, GATE_LANES), gates_r, bias_c, bias_r, head_gain)


def _dil_body(q_ref, k_ref, v_ref, qg_ref, kg_ref, sl_ref, o_ref, qn, kn, m_s, l_s):
    s = q_ref.shape[1]
    n_blocks = s // BLK
    vv, o2 = v_ref.at[0], o_ref.at[0]
    qn[...] = _rms(q_ref[0], qg_ref[...]) * ATTN_SCALE
    kn[...] = _rms(k_ref[0], kg_ref[...])
    slope = sl_ref[0, 0:1, 0:1]
    qq = lax.broadcasted_iota(jnp.int32, (BLK, 2 * BLK), 0)
    kk = lax.broadcasted_iota(jnp.int32, (BLK, 2 * BLK), 1)
    dist = jnp.where(kk < BLK, qq - kk, qq - kk + 2 * BLK)
    in_window = jnp.logical_and(dist >= 0, dist <= BLK)
    prev_lanes = lax.broadcasted_iota(jnp.int32, (1, 2 * BLK), 1) >= BLK
    ones = jnp.ones((2 * BLK, HEAD_DIM), BF16)
    assert n_blocks % DIL_PAR == 0

    for pi, (window, dil) in enumerate(reversed(DILATED_PATTERNS)):
        assert window // dil == BLK and s % (dil * BLK) == 0
        nb = s // (dil * BLK)
        assert nb % DIL_PAR == 0 or DIL_PAR % nb == 0
        bias = jnp.where(in_window, (-float(dil) * slope) * dist.astype(F32), NEG_BIG)
        first, last = pi == 0, pi == len(DILATED_PATTERNS) - 1

        def rows(start, dil=dil):
            return pl.ds(start, BLK) if dil == 1 else pl.ds(start, BLK, stride=dil)

        def group(t0, carry, dil=dil, nb=nb, bias=bias, first=first, last=last, rows=rows):
            cur, qb, kc, vc, no_prev = [], [], [], [], []
            for i in range(DIL_PAR):
                t = t0 * DIL_PAR + i
                r = t // nb
                n = t - r * nb
                c = rows(r + dil * BLK * n)
                cur.append(c)
                qb.append(qn[c, :].astype(BF16))
                kc.append(kn[c, :].astype(BF16))
                vc.append(vv[c, :].astype(BF16))
                no_prev.append(jnp.where(prev_lanes, jnp.where(n > 0, 0.0, NEG_BIG), 0.0))
                if i == 0:
                    p = rows(r + dil * BLK * jnp.maximum(n - 1, 0))
                    kp, vp = [kn[p, :].astype(BF16)], [vv[p, :].astype(BF16)]
                elif i % nb == 0:
                    kp.append(kc[i])
                    vp.append(vc[i])
                else:
                    kp.append(kc[i - 1])
                    vp.append(vc[i - 1])
            k2 = [jnp.concatenate([a, b], axis=0) for a, b in zip(kc, kp)]
            v2 = [jnp.concatenate([a, b], axis=0) for a, b in zip(vc, vp)]
            sc = [jnp.maximum(_dot_nt(q, k) + bias + off, NEG_BIG) for q, k, off in zip(qb, k2, no_prev)]
            m_b = [jnp.max(x, axis=1, keepdims=True) for x in sc]
            pr = [jnp.exp(x - m).astype(BF16) for x, m in zip(sc, m_b)]
            nd = [_dot(p, jnp.concatenate([v, ones], axis=1)) for p, v in zip(pr, v2)]
            res = []
            for c, m, x in zip(cur, m_b, nd):
                num, den = x[:, :HEAD_DIM], x[:, HEAD_DIM:]
                if first:
                    res.append((jnp.broadcast_to(m, (BLK, HEAD_DIM)), den, num))
                else:
                    m_o = m_s[c, :]
                    m_n = jnp.maximum(m_o, m)
                    a_o = jnp.exp(m_o - m_n)
                    a_b = jnp.exp(m - m_n)
                    res.append((m_n, l_s[c, :] * a_o + den * a_b, o2[c, :] * a_o + num * a_b))
            for c, (m_n, l_n, acc) in zip(cur, res):
                if last:
                    o2[c, :] = acc / l_n
                else:
                    m_s[c, :] = m_n
                    l_s[c, :] = l_n
                    o2[c, :] = acc
            return carry

        lax.fori_loop(0, n_blocks // DIL_PAR, group, 0)


def _dilated(pf3, q_gain, k_gain, slopes):
    b, s, _ = pf3.shape

    def col(g):
        return pl.BlockSpec((1, s, HEAD_DIM), lambda bi, hi, g=g: (bi, 0, g * N_HEADS + hi))

    vec = pl.BlockSpec((1, HEAD_DIM), lambda bi, hi: (0, 0))
    return pl.pallas_call(
        _dil_body,
        grid=(b, N_HEADS),
        in_specs=[col(PF_CQ), col(PF_CK), col(PF_CV), vec, vec,
                  pl.BlockSpec((1, 8, HEAD_DIM), lambda bi, hi: (hi, 0, 0))],
        out_specs=pl.BlockSpec((1, s, HEAD_DIM), lambda bi, hi: (bi, 0, hi)),
        out_shape=jax.ShapeDtypeStruct((b, s, GROUP_WIDTH), F32),
        scratch_shapes=[pltpu.VMEM((s, HEAD_DIM), F32)] * 4,
        compiler_params=_params(("parallel", "parallel")),
        name="dilated",
    )(pf3, pf3, pf3, q_gain, k_gain, slopes)


def _sb_body(q_ref, k_ref, v_ref, *rest):
    n_cast = (len(rest) - 2) // 2
    cast_src, o_ref, cast_dst, z_ref = rest[:n_cast], rest[n_cast], rest[n_cast + 1:-1], rest[-1]
    for src, dst in zip(cast_src, cast_dst):
        dst[...] = src[...].astype(BF16)
    i = pl.program_id(1)
    nq = SB_BLK
    qq = lax.broadcasted_iota(jnp.int32, (nq, nq), 0)
    kk = lax.broadcasted_iota(jnp.int32, (nq, nq), 1)
    strict = kk < qq
    after = jnp.where(qq > kk, 1.0, 0.0).astype(BF16)
    after2 = jnp.concatenate([after, after], axis=0)
    o_ref[...] = jnp.zeros_like(o_ref)

    heads = [slice(h * HEAD_DIM, (h + 1) * HEAD_DIM) for h in range(N_HEADS)]

    def logits(j):
        keys = pl.ds(pl.multiple_of(j * nq, nq), nq)
        return [_dot_nt(q_ref[0, :, sl], k_ref[0, keys, sl]) for sl in heads]

    def step(j, gone, masked):
        keys = pl.ds(pl.multiple_of(j * nq, nq), nq)
        zs = [z_ref[h] for h in range(N_HEADS)]
        sps = [jnp.maximum(z, 0.0) + jnp.log2(1.0 + jnp.exp2(-jnp.abs(z))) for z in zs]
        log_beta = [z - sp for z, sp in zip(zs, sps)]
        drops = [jnp.where(strict, sp, 0.0) for sp in sps] if masked else sps
        gone_next = tuple(g + jnp.sum(drop, axis=1, keepdims=True) for g, drop in zip(gone, drops))
        for h, z in enumerate(logits(jnp.maximum(j - 1, 0))):
            z_ref[h] = z
        parts = [jnp.concatenate(_split_bf16(drop, 2), axis=1) for drop in drops]
        laters = [_dot(part, after2) for part in parts]
        ws = [jnp.exp2(lb - later - g) for lb, later, g in zip(log_beta, laters, gone)]
        if masked:
            ws = [jnp.where(strict, w, 0.0) for w in ws]
        pv = [_dot(w.astype(BF16), v_ref[0, keys, sl]) for w, sl in zip(ws, heads)]
        o_ref[0] += jnp.concatenate(pv, axis=1)
        return gone_next

    for h, z in enumerate(logits(i)):
        z_ref[h] = z
    gone = step(i, tuple(jnp.zeros((nq, 1), F32) for _ in range(N_HEADS)), True)
    lax.fori_loop(0, i, lambda jj, g: step(i - 1 - jj, g, False), gone)


def _cast_rows(rows, steps):
    rb = -(-rows // steps)
    rb = -(-rb // BF16_ROWS) * BF16_ROWS
    while rows % rb:
        rb += BF16_ROWS
    return rb


def _stick_breaking(pb3, cast_jobs=()):
    b, s, _ = pb3.shape
    w = GROUP_WIDTH
    nq = s // SB_BLK
    cast_in, cast_out, cast_shapes = [], [], []
    for arr, layer in cast_jobs:
        _, rows, cols = arr.shape
        rb = _cast_rows(rows, b * nq)
        step = lambda bi, qi, last=rows // rb - 1: jnp.minimum(bi * nq + qi, last)
        cast_in.append(pl.BlockSpec((None, rb, cols), lambda bi, qi, f=step, l=layer: (l, f(bi, qi), 0)))
        cast_out.append(pl.BlockSpec((rb, cols), lambda bi, qi, f=step: (f(bi, qi), 0)))
        cast_shapes.append(jax.ShapeDtypeStruct((rows, cols), BF16))
    out = pl.pallas_call(
        _sb_body,
        grid=(b, nq),
        in_specs=[
            pl.BlockSpec((1, SB_BLK, w), lambda bi, qi: (bi, qi, PB_SQ)),
            pl.BlockSpec((1, s, w), lambda bi, qi: (bi, 0, PB_SK)),
            pl.BlockSpec((1, s, w), lambda bi, qi: (bi, 0, PB_SV)),
        ] + cast_in,
        out_specs=[pl.BlockSpec((1, SB_BLK, w), lambda bi, qi: (bi, qi, 0))] + cast_out,
        out_shape=[jax.ShapeDtypeStruct((b, s, w), F32)] + cast_shapes,
        scratch_shapes=[pltpu.VMEM((N_HEADS, SB_BLK, SB_BLK), F32)],
        compiler_params=_params(("arbitrary", "arbitrary")),
        name="stick_breaking",
    )(pb3, pb3, pb3, *[arr for arr, _ in cast_jobs])
    return out[0], tuple(out[1:])


def _cast_body(src, dst):
    dst[...] = src[...].astype(BF16)


def _cast_layer(stacked, layer):
    _, rows, cols = stacked.shape
    rb = _cast_rows(rows, CAST_STEPS)
    return pl.pallas_call(
        _cast_body,
        grid=(rows // rb,),
        in_specs=[pl.BlockSpec((None, rb, cols), lambda i: (layer, i, 0))],
        out_specs=pl.BlockSpec((rb, cols), lambda i: (i, 0)),
        out_shape=jax.ShapeDtypeStruct((rows, cols), BF16),
        compiler_params=_params(("parallel",)),
        name="cast_bf16",
    )(stacked)


def _outproj_body(x_ref, ya_ref, yb_ref, yc_ref, yd_ref, gc_ref, gd_ref, w_ref, o_ref):
    yc = _rms(yc_ref[...], gc_ref[...]).astype(BF16)
    yd = _rms(yd_ref[...], gd_ref[...]).astype(BF16)
    w = GROUP_WIDTH
    acc = _dot(ya_ref[...], w_ref[0:w, :])
    acc += _dot(yb_ref[...], w_ref[w:2 * w, :])
    acc += _dot(yc, w_ref[2 * w:3 * w, :])
    acc += _dot(yd, w_ref[3 * w:4 * w, :])
    o_ref[...] = x_ref[...] + acc


def _outproj(x, ya, yb, yc, yd, gain_c, gain_d, w_out, tm=512):
    m, d = x.shape
    w = GROUP_WIDTH
    yblk = pl.BlockSpec((tm, w), lambda i: (i, 0))
    vec = pl.BlockSpec((1, w), lambda i: (0, 0))
    return pl.pallas_call(
        _outproj_body,
        grid=(m // tm,),
        in_specs=[pl.BlockSpec((tm, d), lambda i: (i, 0)), yblk, yblk, yblk, yblk, vec, vec,
                  pl.BlockSpec((4 * w, d), lambda i: (0, 0))],
        out_specs=pl.BlockSpec((tm, d), lambda i: (i, 0)),
        out_shape=jax.ShapeDtypeStruct((m, d), F32),
        compiler_params=_params(("parallel",)),
        name="outproj",
    )(x, ya, yb, yc, yd, gain_c, gain_d, w_out)


def _row(v):
    return v.reshape(1, -1).astype(F32)


def _split_w_in_body(src, wf_ref, wb_ref, wg_ref):
    w = GROUP_WIDTH
    gate_lo = 6 * w
    gate_hi = gate_lo + 2 * N_HEADS
    x = src[...]
    wf_ref[...] = jnp.concatenate(
        [x[:, 0:2 * w], x[:, 5 * w:6 * w], x[:, gate_hi:gate_hi + 3 * w]], axis=1).astype(BF16)
    wb_ref[...] = jnp.concatenate([x[:, 2 * w:5 * w], x[:, gate_hi + 3 * w:]], axis=1).astype(BF16)
    pad = jnp.zeros((x.shape[0], GATE_LANES - 2 * N_HEADS), F32)
    wg_ref[...] = jnp.concatenate([x[:, gate_lo:gate_hi], pad], axis=1).astype(BF16)


def _split_w_in(w_in, layer):
    _, rows, cols = w_in.shape
    rb = _cast_rows(rows, CAST_STEPS)
    return pl.pallas_call(
        _split_w_in_body,
        grid=(rows // rb,),
        in_specs=[pl.BlockSpec((None, rb, cols), lambda i: (layer, i, 0))],
        out_specs=[pl.BlockSpec((rb, D_HALF), lambda i: (i, 0)),
                   pl.BlockSpec((rb, D_HALF), lambda i: (i, 0)),
                   pl.BlockSpec((rb, GATE_LANES), lambda i: (i, 0))],
        out_shape=[jax.ShapeDtypeStruct((rows, D_HALF), BF16),
                   jax.ShapeDtypeStruct((rows, D_HALF), BF16),
                   jax.ShapeDtypeStruct((rows, GATE_LANES), BF16)],
        compiler_params=_params(("parallel",)),
        name="split_w_in",
    )(w_in)


def kernel(x, ffn1_norm, ffn1_w_gate, ffn1_w_up, ffn1_w_down, mix_norm, w_in, lru_conv_w, lru_conv_b, lru_w_a, lru_b_a, lru_w_x, lru_b_x, lru_lambda, mlstm_ig_bias, mlstm_fg_bias, attn_q_gain, attn_k_gain, group_out_gain, w_out, ffn2_norm, ffn2_w_gate, ffn2_w_up, ffn2_w_down):
    b, s, d = x.shape
    depth = w_in.shape[0]
    m = b * s
    w = GROUP_WIDTH
    slopes = 2.0 ** (-8.0 * jnp.arange(1, N_HEADS + 1, dtype=F32) / N_HEADS)
    slopes = jnp.broadcast_to(slopes[:, None, None], (N_HEADS, 8, HEAD_DIM))
    col_scale = jnp.ones((6, w), F32).at[PB_MK].set(ATTN_SCALE).at[PB_SQ].set(SB_Q_SCALE).reshape(1, D_HALF)

    ffn_w = tuple(_cast_layer(t, 0) for t in (ffn1_w_gate, ffn1_w_up, ffn1_w_down))

    xf = x.reshape(m, d)
    for l in range(depth):
        xf = _ffn(xf, _row(ffn1_norm[l]), ffn_w)

        w_f, w_b, w_gate = _split_w_in(w_in, l)
        pf, pb, gates_c, gates_r = _inproj(xf, _row(mix_norm[l]), w_f, w_b, col_scale, w_gate)
        pf3 = pf.reshape(b, s, D_HALF)
        pb3 = pb.reshape(b, s, D_HALF)

        gains = group_out_gain[l].reshape(4, 1, w)
        ya = _lru(pf3, lru_conv_w[l], _row(lru_conv_b[l]), lru_w_a[l].astype(BF16), _row(lru_b_a[l]),
                  lru_w_x[l].astype(BF16), _row(lru_b_x[l]), _row(lru_lambda[l]), gains[0])

        gate_bias = jnp.concatenate([mlstm_ig_bias[l], mlstm_fg_bias[l]]).astype(F32)
        bias_c = jnp.pad(gate_bias, (0, GATE_LANES - 2 * N_HEADS)).reshape(1, GATE_LANES)
        bias_r = jnp.pad(gate_bias, (0, GATE_ROWS - 2 * N_HEADS)).reshape(GATE_ROWS, 1)
        yb = _mlstm(pf3, pb3, gates_c, gates_r, bias_c, bias_r, gains[1])

        yc = _dilated(pf3, _row(attn_q_gain[l]), _row(attn_k_gain[l]), slopes)
        jobs = [(t, l) for t in (ffn2_w_gate, ffn2_w_up, ffn2_w_down)]
        if l + 1 < depth:
            jobs += [(t, l + 1) for t in (ffn1_w_gate, ffn1_w_up, ffn1_w_down)]
        yd, cast = _stick_breaking(pb3, [(w_out, l)] + jobs)
        w_out_b, ffn2_w, ffn_w = cast[0], cast[1:4], cast[4:]

        xf = _outproj(xf, ya.reshape(m, w), yb.reshape(m, w), yc.reshape(m, w), yd.reshape(m, w),
                      gains[2], gains[3], w_out_b)

        xf = _ffn(xf, _row(ffn2_norm[l]), ffn2_w)
    return xf.reshape(b, s, d)
```

```python
import jax
import jax.numpy as jnp
from jax import lax
from jax.experimental import pallas as pl
from jax.experimental.pallas import tpu as pltpu

F32 = jnp.float32
BF16 = jnp.bfloat16

D_MODEL = 2048
N_HEADS = 4
HEAD_DIM = 128
GROUP_WIDTH = 512
D_FF = 5504
FF_BLK = 128
FF_SUB = 4
FF_TILE = FF_SUB * FF_BLK
LRU_C = 8.0
LRU_CONV_WIDTH = 4
DILATED_PATTERNS = ((128, 1), (512, 4), (2048, 16))
BLK = 128
SB_BLK = 256
DIL_PAR = 8
RMS_EPS = 1e-6
NEG_BIG = -1e30
ATTN_SCALE = HEAD_DIM ** -0.5
SB_Q_SCALE = ATTN_SCALE * 1.4426950408889634
GATE_LANES = 128
GATE_ROWS = 16
SUBLANES = 8
BF16_ROWS = 16
CAST_STEPS = 8
VMEM_LIMIT = 52 * 1024 * 1024
FFN_VMEM_LIMIT = 58 * 1024 * 1024

PF_LX, PF_LG, PF_MO, PF_CQ, PF_CK, PF_CV = range(6)
PB_MQ, PB_MK, PB_MV, PB_SQ, PB_SK, PB_SV = range(6)
D_HALF = 6 * GROUP_WIDTH
PROJ_TILE = 512

_NT = (((1,), (1,)), ((), ()))
_TN = (((0,), (0,)), ((), ()))


def _rms(x, gain):
    return x * lax.rsqrt(jnp.mean(x * x, axis=-1, keepdims=True) + RMS_EPS) * gain


def _softplus(x):
    return jnp.maximum(x, 0.0) + jnp.log(1.0 + jnp.exp(-jnp.abs(x)))


def _log_sigmoid(x):
    return -_softplus(-x)


def _dot(a, b):
    return jnp.dot(a, b, preferred_element_type=F32)


def _dot_nt(a, b):
    return lax.dot_general(a, b, _NT, preferred_element_type=F32)


def _split_bf16(x, parts):
    out = []
    r = x
    for _ in range(parts):
        t = r.astype(BF16)
        out.append(t)
        r = r - t.astype(F32)
    return out


def _params(sem, vmem_limit=None):
    return pltpu.CompilerParams(dimension_semantics=sem, vmem_limit_bytes=vmem_limit or VMEM_LIMIT)


def _ffn_body(x_ref, g_ref, *refs):
    wg_refs, wu_refs, wd_refs = refs[:FF_SUB], refs[FF_SUB:2 * FF_SUB], refs[2 * FF_SUB:3 * FF_SUB]
    o_ref, h_ref = refs[3 * FF_SUB:]
    j = pl.program_id(1)

    @pl.when(j == 0)
    def _():
        h_ref[...] = _rms(x_ref[...], g_ref[...]).astype(BF16)
        o_ref[...] = jnp.zeros_like(o_ref)

    h = h_ref[...]
    g = _dot(h, jnp.concatenate([r[...] for r in wg_refs], axis=1))
    u = _dot(h, jnp.concatenate([r[...] for r in wu_refs], axis=1))
    col = lax.broadcasted_iota(jnp.int32, (1, FF_TILE), 1)
    a = jnp.where(col < D_FF - j * FF_TILE, (g * jax.nn.sigmoid(g)) * u, 0.0)
    o_ref[...] += _dot(a.astype(BF16), jnp.concatenate([r[...] for r in wd_refs], axis=0))

    @pl.when(j == pl.num_programs(1) - 1)
    def _():
        o_ref[...] = x_ref[...] + 0.5 * o_ref[...]


def _ffn(x, gain, weights, tm=1024):
    wg, wu, wd = weights
    m, d = x.shape
    last_blk = D_FF // FF_BLK - 1

    def blk(k):
        return lambda i, j: jnp.minimum(j * FF_SUB + k, last_blk)

    cols = [pl.BlockSpec((d, FF_BLK), lambda i, j, f=blk(k): (0, f(i, j))) for k in range(FF_SUB)]
    rows = [pl.BlockSpec((FF_BLK, d), lambda i, j, f=blk(k): (f(i, j), 0)) for k in range(FF_SUB)]
    return pl.pallas_call(
        _ffn_body,
        grid=(m // tm, pl.cdiv(D_FF, FF_TILE)),
        in_specs=[pl.BlockSpec((tm, d), lambda i, j: (i, 0)),
                  pl.BlockSpec((1, d), lambda i, j: (0, 0))] + cols + cols + rows,
        out_specs=pl.BlockSpec((tm, d), lambda i, j: (i, 0)),
        out_shape=jax.ShapeDtypeStruct((m, d), F32),
        scratch_shapes=[pltpu.VMEM((tm, d), BF16)],
        compiler_params=_params(("parallel", "arbitrary"), FFN_VMEM_LIMIT),
        name="ffn",
    )(x, gain, *([wg] * FF_SUB), *([wu] * FF_SUB), *([wd] * FF_SUB))


def _inproj_body(x_ref, g_ref, wf_ref, wb_ref, sc_ref, wgc_ref,
                 pf_ref, pb_ref, gc_ref, gr_ref, h_ref):
    j = pl.program_id(1)

    @pl.when(j == 0)
    def _():
        h = _rms(x_ref[...], g_ref[...]).astype(BF16)
        h_ref[...] = h
        gates = _dot(h, wgc_ref[...])
        gc_ref[...] = gates
        gr_ref[...] = gates.T[:GATE_ROWS, :]

    h = h_ref[...]
    pf_ref[...] = _dot(h, wf_ref[...])
    pb_ref[...] = (_dot(h, wb_ref[...]) * sc_ref[...]).astype(BF16)


def _inproj(x, gain, w_main, col_scale, w_gate_c, tm=1024):
    m, d = x.shape
    tn = PROJ_TILE
    return pl.pallas_call(
        _inproj_body,
        grid=(m // tm, D_HALF // tn),
        in_specs=[
            pl.BlockSpec((tm, d), lambda i, j: (i, 0)),
            pl.BlockSpec((1, d), lambda i, j: (0, 0)),
            pl.BlockSpec((d, tn), lambda i, j: (0, j)),
            pl.BlockSpec((d, tn), lambda i, j: (0, j + D_HALF // tn)),
            pl.BlockSpec((1, tn), lambda i, j: (0, j)),
            pl.BlockSpec((d, GATE_LANES), lambda i, j: (0, 0)),
        ],
        out_specs=[
            pl.BlockSpec((tm, tn), lambda i, j: (i, j)),
            pl.BlockSpec((tm, tn), lambda i, j: (i, j)),
            pl.BlockSpec((tm, GATE_LANES), lambda i, j: (i, 0)),
            pl.BlockSpec((GATE_ROWS, tm), lambda i, j: (0, i)),
        ],
        out_shape=[
            jax.ShapeDtypeStruct((m, D_HALF), F32),
            jax.ShapeDtypeStruct((m, D_HALF), BF16),
            jax.ShapeDtypeStruct((m, GATE_LANES), F32),
            jax.ShapeDtypeStruct((GATE_ROWS, m), F32),
        ],
        scratch_shapes=[pltpu.VMEM((tm, d), BF16)],
        compiler_params=_params(("parallel", "arbitrary")),
        name="inproj",
    )(x, gain, w_main, w_main, col_scale, w_gate_c)


def _lru_body(xr_ref, gate_ref, cw_ref, cb_ref, wa_ref, ba_ref, wx_ref, bx_ref, lam_ref,
              gain_ref, o_ref, xbuf, hcar):
    t = pl.program_id(1)
    tt = xr_ref.shape[1]
    width = xr_ref.shape[2]
    pad = 8

    @pl.when(t == 0)
    def _():
        xbuf[0:pad, :] = jnp.zeros((pad, width), F32)
        hcar[...] = jnp.zeros_like(hcar)

    xr = xr_ref[0]
    xbuf[pad:pad + tt, :] = xr
    xc = cb_ref[...]
    for j in range(LRU_CONV_WIDTH):
        off = pad - (LRU_CONV_WIDTH - 1) + j
        xc = xc + cw_ref[j:j + 1, :] * xbuf[pl.ds(off, tt), :]
    xbuf[0:pad, :] = xr[tt - pad:tt, :]

    xcb = xc.astype(BF16)
    ra, rx = [], []
    for n in range(N_HEADS):
        blk = xcb[:, n * HEAD_DIM:(n + 1) * HEAD_DIM]
        ra.append(_dot(blk, wa_ref[n]))
        rx.append(_dot(blk, wx_ref[n]))
    r = jax.nn.sigmoid(jnp.concatenate(ra, axis=1) + ba_ref[...])
    i = jax.nn.sigmoid(jnp.concatenate(rx, axis=1) + bx_ref[...])
    log_a = (-LRU_C * _softplus(-lam_ref[...])) * r
    a = jnp.exp(log_a)
    u = jnp.sqrt(-jnp.tanh(log_a) * (a * a + 1.0)) * (i * xc)

    sub = lax.broadcasted_iota(jnp.int32, (tt, width), 0) % SUBLANES
    s = 1
    while s < SUBLANES:
        keep = sub >= s
        a_sh = jnp.where(keep, pltpu.roll(a, s, 0), 1.0)
        u_sh = jnp.where(keep, pltpu.roll(u, s, 0), 0.0)
        u = a * u_sh + u
        a = a * a_sh
        s *= 2
    carry = hcar[...]
    groups = []
    for g in range(tt // SUBLANES):
        rows = slice(g * SUBLANES, (g + 1) * SUBLANES)
        hg = u[rows, :] + a[rows, :] * carry
        groups.append(hg)
        carry = hg[SUBLANES - 1:SUBLANES, :]
    h = jnp.concatenate(groups, axis=0)
    hcar[...] = carry

    y = h * jax.nn.gelu(gate_ref[0])
    o_ref[0] = _rms(y, gain_ref[...]).astype(BF16)


def _lru(pf3, conv_w, conv_b, w_a, b_a, w_x, b_x, lam, gain, tt=512):
    b, s, _ = pf3.shape
    w = GROUP_WIDTH
    vec = pl.BlockSpec((1, w), lambda bi, ti: (0, 0))
    mat = pl.BlockSpec((N_HEADS, HEAD_DIM, HEAD_DIM), lambda bi, ti: (0, 0, 0))
    return pl.pallas_call(
        _lru_body,
        grid=(b, s // tt),
        in_specs=[
            pl.BlockSpec((1, tt, w), lambda bi, ti: (bi, ti, PF_LX)),
            pl.BlockSpec((1, tt, w), lambda bi, ti: (bi, ti, PF_LG)),
            pl.BlockSpec((LRU_CONV_WIDTH, w), lambda bi, ti: (0, 0)),
            vec, mat, vec, mat, vec, vec, vec,
        ],
        out_specs=pl.BlockSpec((1, tt, w), lambda bi, ti: (bi, ti, 0)),
        out_shape=jax.ShapeDtypeStruct((b, s, w), BF16),
        scratch_shapes=[pltpu.VMEM((tt + 8, w), F32), pltpu.VMEM((1, w), F32)],
        compiler_params=_params(("parallel", "arbitrary")),
        name="rglru",
    )(pf3, pf3, conv_w, conv_b, w_a, b_a, w_x, b_x, lam, gain)


def _mlstm_body(q_ref, k_ref, v_ref, og_ref, gc_ref, gr_ref, bc_ref, br_ref, hg_ref,
                out_ref, c_ref, m_ref):
    c = pl.program_id(1)
    ln = q_ref.shape[1]

    @pl.when(c == 0)
    def _():
        c_ref[...] = jnp.zeros_like(c_ref)
        m_ref[...] = jnp.zeros_like(m_ref)

    gcol = gc_ref[0] + bc_ref[...]
    grow = gr_ref[...] + br_ref[...]
    lf_col = _log_sigmoid(gcol)
    lf_row = _log_sigmoid(grow)
    ri = lax.broadcasted_iota(jnp.int32, (ln, ln), 0)
    ci = lax.broadcasted_iota(jnp.int32, (ln, ln), 1)
    causal = ri >= ci
    tri_l = jnp.where(causal, 1.0, 0.0).astype(BF16)
    tri_u = jnp.where(ri <= ci, 1.0, 0.0).astype(BF16)
    b_col = sum(_dot(tri_l, part) for part in _split_bf16(lf_col, 3))
    b_row = sum(_dot(part, tri_u) for part in _split_bf16(lf_row, 3))

    ones_blk = jnp.ones((ln, HEAD_DIM), BF16)
    reps = ln // HEAD_DIM

    def wide(x):
        return jnp.concatenate([x] * reps, axis=1)

    heads = [slice(h * HEAD_DIM, (h + 1) * HEAD_DIM) for h in range(N_HEADS)]
    qs = [q_ref[0, :, sl] for sl in heads]
    ks = [k_ref[0, :, sl] for sl in heads]
    v_augs = [jnp.concatenate([v_ref[0, :, sl], ones_blk], axis=1) for sl in heads]
    c_augs = [c_ref[h] for h in range(N_HEADS)]

    decay_w, m_ts, w_inters, m_nexts, decays, kws = [], [], [], [], [], []
    for h in range(N_HEADS):
        bc = jnp.broadcast_to(b_col[:, N_HEADS + h:N_HEADS + h + 1], (ln, HEAD_DIM))
        ig = jnp.broadcast_to(gcol[:, h:h + 1], (ln, HEAD_DIM))
        brow = b_row[N_HEADS + h:N_HEADS + h + 1, :]
        igr = grow[h:h + 1, :]
        m_run = m_ref[h:h + 1, :]
        b_last = bc[ln - 1:ln, :]
        log_d = jnp.where(causal, wide(bc) - brow + igr, NEG_BIG)
        inter = bc + m_run
        m_t = jnp.maximum(inter, jnp.max(log_d, axis=1, keepdims=True))
        decay_w.append(jnp.exp(log_d - wide(m_t)))
        m_ts.append(m_t)
        w_inters.append(jnp.exp(inter - m_t))
        log_w = b_last - bc + ig
        m_next = jnp.maximum(b_last + m_run, jnp.max(log_w, axis=0, keepdims=True))
        m_nexts.append(m_next)
        decays.append(jnp.exp(b_last + m_run - m_next))
        kws.append((ks[h].astype(F32) * jnp.exp(log_w - m_next)).astype(BF16))

    qk = [_dot_nt(q, k) for q, k in zip(qs, ks)]
    qc = [_dot(q, c_aug.astype(BF16)) for q, c_aug in zip(qs, c_augs)]
    smats = [(s * d).astype(BF16) for s, d in zip(qk, decay_w)]
    intra = [_dot(s, v_aug) for s, v_aug in zip(smats, v_augs)]
    upd = [lax.dot_general(kw, v_aug, _TN, preferred_element_type=F32) for kw, v_aug in zip(kws, v_augs)]

    for h, sl in enumerate(heads):
        num = intra[h][:, :HEAD_DIM] + w_inters[h] * qc[h][:, :HEAD_DIM]
        den = intra[h][:, HEAD_DIM:] + w_inters[h] * qc[h][:, HEAD_DIM:]
        hh = num / jnp.maximum(jnp.abs(den), jnp.exp(-m_ts[h]))
        c_ref[h] = jnp.concatenate([decays[h]] * 2, axis=1) * c_augs[h] + upd[h]
        m_ref[h:h + 1, :] = m_nexts[h]
        hn = _rms(hh, hg_ref[:, sl])
        out_ref[0, :, sl] = (hn * jax.nn.sigmoid(og_ref[0, :, sl])).astype(BF16)


def _mlstm(pf3, pb3, gates_c, gates_r, bias_c, bias_r, head_gain, ln=256):
    b, s, _ = pf3.shape
    nc = s // ln
    w = GROUP_WIDTH

    def col(g):
        return pl.BlockSpec((1, ln, w), lambda bi, ci, g=g: (bi, ci, g))

    return pl.pallas_call(
        _mlstm_body,
        grid=(b, nc),
        in_specs=[
            col(PB_MQ), col(PB_MK), col(PB_MV), col(PF_MO),
            pl.BlockSpec((1, ln, GATE_LANES), lambda bi, ci: (bi, ci, 0)),
            pl.BlockSpec((GATE_ROWS, ln), lambda bi, ci: (0, bi * nc + ci)),
            pl.BlockSpec((1, GATE_LANES), lambda bi, ci: (0, 0)),
            pl.BlockSpec((GATE_ROWS, 1), lambda bi, ci: (0, 0)),
            pl.BlockSpec((1, w), lambda bi, ci: (0, 0)),
        ],
        out_specs=pl.BlockSpec((1, ln, w), lambda bi, ci: (bi, ci, 0)),
        out_shape=jax.ShapeDtypeStruct((b, s, w), BF16),
        scratch_shapes=[pltpu.VMEM((N_HEADS, HEAD_DIM, 2 * HEAD_DIM), F32),
                        pltpu.VMEM((8, 128), F32)],
        compiler_params=_params(("parallel", "arbitrary")),
        name="mlstm",
    )(pb3, pb3, pb3, pf3, gates_c.reshape(b, s, GATE_LANES), gates_r, bias_c, bias_r, head_gain)


def _dil_body(q_ref, k_ref, v_ref, qg_ref, kg_ref, sl_ref, o_ref, qn, kn, m_s, l_s):
    s = q_ref.shape[1]
    n_blocks = s // BLK
    vv, o2 = v_ref.at[0], o_ref.at[0]
    qn[...] = _rms(q_ref[0], qg_ref[...]) * ATTN_SCALE
    kn[...] = _rms(k_ref[0], kg_ref[...])
    slope = sl_ref[0, 0:1, 0:1]
    qq = lax.broadcasted_iota(jnp.int32, (BLK, 2 * BLK), 0)
    kk = lax.broadcasted_iota(jnp.int32, (BLK, 2 * BLK), 1)
    dist = jnp.where(kk < BLK, qq - kk, qq - kk + 2 * BLK)
    in_window = jnp.logical_and(dist >= 0, dist <= BLK)
    prev_lanes = lax.broadcasted_iota(jnp.int32, (1, 2 * BLK), 1) >= BLK
    ones = jnp.ones((2 * BLK, HEAD_DIM), BF16)
    assert n_blocks % DIL_PAR == 0

    for pi, (window, dil) in enumerate(reversed(DILATED_PATTERNS)):
        assert window // dil == BLK and s % (dil * BLK) == 0
        nb = s // (dil * BLK)
        assert nb % DIL_PAR == 0 or DIL_PAR % nb == 0
        bias = jnp.where(in_window, (-float(dil) * slope) * dist.astype(F32), NEG_BIG)
        first, last = pi == 0, pi == len(DILATED_PATTERNS) - 1

        def rows(start, dil=dil):
            return pl.ds(start, BLK) if dil == 1 else pl.ds(start, BLK, stride=dil)

        def group(t0, carry, dil=dil, nb=nb, bias=bias, first=first, last=last, rows=rows):
            cur, qb, kc, vc, no_prev = [], [], [], [], []
            for i in range(DIL_PAR):
                t = t0 * DIL_PAR + i
                r = t // nb
                n = t - r * nb
                c = rows(r + dil * BLK * n)
                cur.append(c)
                qb.append(qn[c, :].astype(BF16))
                kc.append(kn[c, :].astype(BF16))
                vc.append(vv[c, :].astype(BF16))
                no_prev.append(jnp.where(prev_lanes, jnp.where(n > 0, 0.0, NEG_BIG), 0.0))
                if i == 0:
                    p = rows(r + dil * BLK * jnp.maximum(n - 1, 0))
                    kp, vp = [kn[p, :].astype(BF16)], [vv[p, :].astype(BF16)]
                elif i % nb == 0:
                    kp.append(kc[i])
                    vp.append(vc[i])
                else:
                    kp.append(kc[i - 1])
                    vp.append(vc[i - 1])
            k2 = [jnp.concatenate([a, b], axis=0) for a, b in zip(kc, kp)]
            v2 = [jnp.concatenate([a, b], axis=0) for a, b in zip(vc, vp)]
            sc = [jnp.maximum(_dot_nt(q, k) + bias + off, NEG_BIG) for q, k, off in zip(qb, k2, no_prev)]
            m_b = [jnp.max(x, axis=1, keepdims=True) for x in sc]
            pr = [jnp.exp(x - m).astype(BF16) for x, m in zip(sc, m_b)]
            nd = [_dot(p, jnp.concatenate([v, ones], axis=1)) for p, v in zip(pr, v2)]
            res = []
            for c, m, x in zip(cur, m_b, nd):
                num, den = x[:, :HEAD_DIM], x[:, HEAD_DIM:]
                if first:
                    res.append((jnp.broadcast_to(m, (BLK, HEAD_DIM)), den, num))
                else:
                    m_o = m_s[c, :]
                    m_n = jnp.maximum(m_o, m)
                    a_o = jnp.exp(m_o - m_n)
                    a_b = jnp.exp(m - m_n)
                    res.append((m_n, l_s[c, :] * a_o + den * a_b, o2[c, :] * a_o + num * a_b))
            for c, (m_n, l_n, acc) in zip(cur, res):
                if last:
                    o2[c, :] = acc / l_n
                else:
                    m_s[c, :] = m_n
                    l_s[c, :] = l_n
                    o2[c, :] = acc
            return carry

        lax.fori_loop(0, n_blocks // DIL_PAR, group, 0)


def _dilated(pf3, q_gain, k_gain, slopes):
    b, s, _ = pf3.shape

    def col(g):
        return pl.BlockSpec((1, s, HEAD_DIM), lambda bi, hi, g=g: (bi, 0, g * N_HEADS + hi))

    vec = pl.BlockSpec((1, HEAD_DIM), lambda bi, hi: (0, 0))
    return pl.pallas_call(
        _dil_body,
        grid=(b, N_HEADS),
        in_specs=[col(PF_CQ), col(PF_CK), col(PF_CV), vec, vec,
                  pl.BlockSpec((1, 8, HEAD_DIM), lambda bi, hi: (hi, 0, 0))],
        out_specs=pl.BlockSpec((1, s, HEAD_DIM), lambda bi, hi: (bi, 0, hi)),
        out_shape=jax.ShapeDtypeStruct((b, s, GROUP_WIDTH), F32),
        scratch_shapes=[pltpu.VMEM((s, HEAD_DIM), F32)] * 4,
        compiler_params=_params(("parallel", "parallel")),
        name="dilated",
    )(pf3, pf3, pf3, q_gain, k_gain, slopes)


def _sb_body(q_ref, k_ref, v_ref, *rest):
    n_cast = (len(rest) - 2) // 2
    cast_src, o_ref, cast_dst, z_ref = rest[:n_cast], rest[n_cast], rest[n_cast + 1:-1], rest[-1]
    for src, dst in zip(cast_src, cast_dst):
        dst[...] = src[...].astype(BF16)
    i = pl.program_id(1)
    nq = SB_BLK
    qq = lax.broadcasted_iota(jnp.int32, (nq, nq), 0)
    kk = lax.broadcasted_iota(jnp.int32, (nq, nq), 1)
    strict = kk < qq
    after = jnp.where(qq > kk, 1.0, 0.0).astype(BF16)
    after2 = jnp.concatenate([after, after], axis=0)
    o_ref[...] = jnp.zeros_like(o_ref)

    heads = [slice(h * HEAD_DIM, (h + 1) * HEAD_DIM) for h in range(N_HEADS)]

    def logits(j):
        keys = pl.ds(pl.multiple_of(j * nq, nq), nq)
        return [_dot_nt(q_ref[0, :, sl], k_ref[0, keys, sl]) for sl in heads]

    def step(j, gone, masked):
        keys = pl.ds(pl.multiple_of(j * nq, nq), nq)
        zs = [z_ref[h] for h in range(N_HEADS)]
        sps = [jnp.maximum(z, 0.0) + jnp.log2(1.0 + jnp.exp2(-jnp.abs(z))) for z in zs]
        log_beta = [z - sp for z, sp in zip(zs, sps)]
        drops = [jnp.where(strict, sp, 0.0) for sp in sps] if masked else sps
        gone_next = tuple(g + jnp.sum(drop, axis=1, keepdims=True) for g, drop in zip(gone, drops))
        for h, z in enumerate(logits(jnp.maximum(j - 1, 0))):
            z_ref[h] = z
        parts = [jnp.concatenate(_split_bf16(drop, 2), axis=1) for drop in drops]
        laters = [_dot(part, after2) for part in parts]
        ws = [jnp.exp2(lb - later - g) for lb, later, g in zip(log_beta, laters, gone)]
        if masked:
            ws = [jnp.where(strict, w, 0.0) for w in ws]
        pv = [_dot(w.astype(BF16), v_ref[0, keys, sl]) for w, sl in zip(ws, heads)]
        o_ref[0] += jnp.concatenate(pv, axis=1)
        return gone_next

    for h, z in enumerate(logits(i)):
        z_ref[h] = z
    gone = step(i, tuple(jnp.zeros((nq, 1), F32) for _ in range(N_HEADS)), True)
    lax.fori_loop(0, i, lambda jj, g: step(i - 1 - jj, g, False), gone)


def _cast_rows(rows, steps):
    rb = -(-rows // steps)
    rb = -(-rb // BF16_ROWS) * BF16_ROWS
    while rows % rb:
        rb += BF16_ROWS
    return rb


def _stick_breaking(pb3, cast_jobs=()):
    b, s, _ = pb3.shape
    w = GROUP_WIDTH
    nq = s // SB_BLK
    cast_in, cast_out, cast_shapes = [], [], []
    for arr, layer in cast_jobs:
        _, rows, cols = arr.shape
        rb = _cast_rows(rows, b * nq)
        step = lambda bi, qi, last=rows // rb - 1: jnp.minimum(bi * nq + qi, last)
        cast_in.append(pl.BlockSpec((None, rb, cols), lambda bi, qi, f=step, l=layer: (l, f(bi, qi), 0)))
        cast_out.append(pl.BlockSpec((rb, cols), lambda bi, qi, f=step: (f(bi, qi), 0)))
        cast_shapes.append(jax.ShapeDtypeStruct((rows, cols), BF16))
    out = pl.pallas_call(
        _sb_body,
        grid=(b, nq),
        in_specs=[
            pl.BlockSpec((1, SB_BLK, w), lambda bi, qi: (bi, qi, PB_SQ)),
            pl.BlockSpec((1, s, w), lambda bi, qi: (bi, 0, PB_SK)),
            pl.BlockSpec((1, s, w), lambda bi, qi: (bi, 0, PB_SV)),
        ] + cast_in,
        out_specs=[pl.BlockSpec((1, SB_BLK, w), lambda bi, qi: (bi, qi, 0))] + cast_out,
        out_shape=[jax.ShapeDtypeStruct((b, s, w), F32)] + cast_shapes,
        scratch_shapes=[pltpu.VMEM((N_HEADS, SB_BLK, SB_BLK), F32)],
        compiler_params=_params(("arbitrary", "arbitrary")),
        name="stick_breaking",
    )(pb3, pb3, pb3, *[arr for arr, _ in cast_jobs])
    return out[0], tuple(out[1:])


def _cast_body(src, dst):
    dst[...] = src[...].astype(BF16)


def _cast_layer(stacked, layer):
    _, rows, cols = stacked.shape
    rb = _cast_rows(rows, CAST_STEPS)
    return pl.pallas_call(
        _cast_body,
        grid=(rows // rb,),
        in_specs=[pl.BlockSpec((None, rb, cols), lambda i: (layer, i, 0))],
        out_specs=pl.BlockSpec((rb, cols), lambda i: (i, 0)),
        out_shape=jax.ShapeDtypeStruct((rows, cols), BF16),
        compiler_params=_params(("parallel",)),
        name="cast_bf16",
    )(stacked)


def _outproj_body(x_ref, ya_ref, yb_ref, yc_ref, yd_ref, gc_ref, gd_ref, w_ref, o_ref):
    yc = _rms(yc_ref[...], gc_ref[...]).astype(BF16)
    yd = _rms(yd_ref[...], gd_ref[...]).astype(BF16)
    w = GROUP_WIDTH
    acc = _dot(ya_ref[...], w_ref[0:w, :])
    acc += _dot(yb_ref[...], w_ref[w:2 * w, :])
    acc += _dot(yc, w_ref[2 * w:3 * w, :])
    acc += _dot(yd, w_ref[3 * w:4 * w, :])
    o_ref[...] = x_ref[...] + acc


def _outproj(x, ya, yb, yc, yd, gain_c, gain_d, w_out, tm=512):
    m, d = x.shape
    w = GROUP_WIDTH
    yblk = pl.BlockSpec((tm, w), lambda i: (i, 0))
    vec = pl.BlockSpec((1, w), lambda i: (0, 0))
    return pl.pallas_call(
        _outproj_body,
        grid=(m // tm,),
        in_specs=[pl.BlockSpec((tm, d), lambda i: (i, 0)), yblk, yblk, yblk, yblk, vec, vec,
                  pl.BlockSpec((4 * w, d), lambda i: (0, 0))],
        out_specs=pl.BlockSpec((tm, d), lambda i: (i, 0)),
        out_shape=jax.ShapeDtypeStruct((m, d), F32),
        compiler_params=_params(("parallel",)),
        name="outproj",
    )(x, ya, yb, yc, yd, gain_c, gain_d, w_out)


def _row(v):
    return v.reshape(1, -1).astype(F32)


_W_IN_GROUP_ROWS = (0, 512, 2560, 3080, 3592, 4104, 1024, 1536, 2048, 4616, 5128, 5640)
_W_IN_GATE_ROW = 6 * GROUP_WIDTH


def _w_in_group_row(g):
    row = jnp.int32(_W_IN_GROUP_ROWS[0])
    for k in range(1, len(_W_IN_GROUP_ROWS)):
        row = jnp.where(g == k, _W_IN_GROUP_ROWS[k], row)
    return row


def _split_w_in_body(grp_ref, gate_ref, w_ref, wg_ref):
    w_ref[...] = grp_ref[...].T.astype(BF16)

    @pl.when(pl.program_id(0) == 0)
    def _():
        gates = gate_ref[...]
        pad = jnp.zeros((GATE_LANES - gates.shape[0], gates.shape[1]), F32)
        wg_ref[...] = jnp.concatenate([gates, pad], axis=0).T.astype(BF16)


def _split_w_in(w_in_t, layer):
    n_layers, d_in, d = w_in_t.shape
    n_groups = len(_W_IN_GROUP_ROWS)
    base = layer * d_in
    assert base % SUBLANES == 0 and all(r % SUBLANES == 0 for r in _W_IN_GROUP_ROWS)
    flat = w_in_t.reshape(n_layers * d_in, d)
    return pl.pallas_call(
        _split_w_in_body,
        grid=(n_groups,),
        in_specs=[pl.BlockSpec((pl.Element(GROUP_WIDTH), pl.Element(d)),
                               lambda g: (pl.multiple_of(base + _w_in_group_row(g), SUBLANES), 0)),
                  pl.BlockSpec((pl.Element(2 * N_HEADS), pl.Element(d)),
                               lambda g: (base + _W_IN_GATE_ROW, 0))],
        out_specs=[pl.BlockSpec((d, GROUP_WIDTH), lambda g: (0, g)),
                   pl.BlockSpec((d, GATE_LANES), lambda g: (0, 0))],
        out_shape=[jax.ShapeDtypeStruct((d, n_groups * GROUP_WIDTH), BF16),
                   jax.ShapeDtypeStruct((d, GATE_LANES), BF16)],
        compiler_params=_params(("arbitrary",)),
        name="split_w_in",
    )(flat, flat)


def kernel(x, ffn1_norm, ffn1_w_gate, ffn1_w_up, ffn1_w_down, mix_norm, w_in, lru_conv_w, lru_conv_b, lru_w_a, lru_b_a, lru_w_x, lru_b_x, lru_lambda, mlstm_ig_bias, mlstm_fg_bias, attn_q_gain, attn_k_gain, group_out_gain, w_out, ffn2_norm, ffn2_w_gate, ffn2_w_up, ffn2_w_down):
    b, s, d = x.shape
    depth = w_in.shape[0]
    m = b * s
    w = GROUP_WIDTH
    slopes = 2.0 ** (-8.0 * jnp.arange(1, N_HEADS + 1, dtype=F32) / N_HEADS)
    slopes = jnp.broadcast_to(slopes[:, None, None], (N_HEADS, 8, HEAD_DIM))
    col_scale = jnp.ones((6, w), F32).at[PB_MK].set(ATTN_SCALE).at[PB_SQ].set(SB_Q_SCALE).reshape(1, D_HALF)

    ffn_w = tuple(_cast_layer(t, 0) for t in (ffn1_w_gate, ffn1_w_up, ffn1_w_down))

    w_in_t = jnp.swapaxes(w_in, 1, 2)

    xf = x.reshape(m, d)
    for l in range(depth):
        xf = _ffn(xf, _row(ffn1_norm[l]), ffn_w)

        w_main, w_gate = _split_w_in(w_in_t, l)
        pf, pb, gates_c, gates_r = _inproj(xf, _row(mix_norm[l]), w_main, col_scale, w_gate)
        pf3 = pf.reshape(b, s, D_HALF)
        pb3 = pb.reshape(b, s, D_HALF)

        gains = group_out_gain[l].reshape(4, 1, w)
        ya = _lru(pf3, lru_conv_w[l], _row(lru_conv_b[l]), lru_w_a[l].astype(BF16), _row(lru_b_a[l]),
                  lru_w_x[l].astype(BF16), _row(lru_b_x[l]), _row(lru_lambda[l]), gains[0])

        gate_bias = jnp.concatenate([mlstm_ig_bias[l], mlstm_fg_bias[l]]).astype(F32)
        bias_c = jnp.pad(gate_bias, (0, GATE_LANES - 2 * N_HEADS)).reshape(1, GATE_LANES)
        bias_r = jnp.pad(gate_bias, (0, GATE_ROWS - 2 * N_HEADS)).reshape(GATE_ROWS, 1)
        yb = _mlstm(pf3, pb3, gates_c, gates_r, bias_c, bias_r, gains[1])

        yc = _dilated(pf3, _row(attn_q_gain[l]), _row(attn_k_gain[l]), slopes)
        jobs = [(t, l) for t in (ffn2_w_gate, ffn2_w_up, ffn2_w_down)]
        if l + 1 < depth:
            jobs += [(t, l + 1) for t in (ffn1_w_gate, ffn1_w_up, ffn1_w_down)]
        yd, cast = _stick_breaking(pb3, [(w_out, l)] + jobs)
        w_out_b, ffn2_w, ffn_w = cast[0], cast[1:4], cast[4:]

        xf = _outproj(xf, ya.reshape(m, w), yb.reshape(m, w), yc.reshape(m, w), yd.reshape(m, w),
                      gains[2], gains[3], w_out_b)

        xf = _ffn(xf, _row(ffn2_norm[l]), ffn2_w)
    return xf.reshape(b, s, d)
```

```python
import jax
import jax.numpy as jnp
from jax import lax
from jax.experimental import pallas as pl
from jax.experimental.pallas import tpu as pltpu

F32 = jnp.float32
BF16 = jnp.bfloat16

D_MODEL = 2048
N_HEADS = 4
HEAD_DIM = 128
GROUP_WIDTH = 512
D_FF = 5504
FF_BLK = 128
FF_SUB = 4
FF_TILE = FF_SUB * FF_BLK
LRU_C = 8.0
LRU_CONV_WIDTH = 4
DILATED_PATTERNS = ((128, 1), (512, 4), (2048, 16))
BLK = 128
SB_Q = 256
SB_K = 256
SB_HEADS = 4
DIL_PAR = 8
RMS_EPS = 1e-6
NEG_BIG = -1e30
ATTN_SCALE = HEAD_DIM ** -0.5
SB_Q_SCALE = ATTN_SCALE * 1.4426950408889634
GATE_LANES = 128
GATE_ROWS = 16
SUBLANES = 8
BF16_ROWS = 16
CAST_STEPS = 8
VMEM_LIMIT = 52 * 1024 * 1024
FFN_VMEM_LIMIT = 58 * 1024 * 1024

PF_LX, PF_LG, PF_MO, PF_CQ, PF_CK, PF_CV = range(6)
PB_MQ, PB_MK, PB_MV, PB_SQ, PB_SK, PB_SV = range(6)
D_HALF = 6 * GROUP_WIDTH
PROJ_TILE = 512

_NT = (((1,), (1,)), ((), ()))
_TN = (((0,), (0,)), ((), ()))


def _rms(x, gain):
    return x * lax.rsqrt(jnp.mean(x * x, axis=-1, keepdims=True) + RMS_EPS) * gain


def _softplus(x):
    return jnp.maximum(x, 0.0) + jnp.log(1.0 + jnp.exp(-jnp.abs(x)))


def _log_sigmoid(x):
    return -_softplus(-x)


def _dot(a, b):
    return jnp.dot(a, b, preferred_element_type=F32)


def _dot_nt(a, b):
    return lax.dot_general(a, b, _NT, preferred_element_type=F32)


def _split_bf16(x, parts):
    out = []
    r = x
    for _ in range(parts):
        t = r.astype(BF16)
        out.append(t)
        r = r - t.astype(F32)
    return out


def _params(sem, vmem_limit=None):
    return pltpu.CompilerParams(dimension_semantics=sem, vmem_limit_bytes=vmem_limit or VMEM_LIMIT)


def _ffn_body(x_ref, g_ref, *refs):
    wg_refs, wu_refs, wd_refs = refs[:FF_SUB], refs[FF_SUB:2 * FF_SUB], refs[2 * FF_SUB:3 * FF_SUB]
    o_ref, h_ref = refs[3 * FF_SUB:]
    j = pl.program_id(1)

    @pl.when(j == 0)
    def _():
        h_ref[...] = _rms(x_ref[...], g_ref[...]).astype(BF16)
        o_ref[...] = jnp.zeros_like(o_ref)

    h = h_ref[...]
    g = _dot(h, jnp.concatenate([r[...] for r in wg_refs], axis=1))
    u = _dot(h, jnp.concatenate([r[...] for r in wu_refs], axis=1))
    col = lax.broadcasted_iota(jnp.int32, (1, FF_TILE), 1)
    a = jnp.where(col < D_FF - j * FF_TILE, (g * jax.nn.sigmoid(g)) * u, 0.0)
    o_ref[...] += _dot(a.astype(BF16), jnp.concatenate([r[...] for r in wd_refs], axis=0))

    @pl.when(j == pl.num_programs(1) - 1)
    def _():
        o_ref[...] = x_ref[...] + 0.5 * o_ref[...]


def _ffn(x, gain, weights, tm=1024):
    wg, wu, wd = weights
    m, d = x.shape
    last_blk = D_FF // FF_BLK - 1

    def blk(k):
        return lambda i, j: jnp.minimum(j * FF_SUB + k, last_blk)

    cols = [pl.BlockSpec((d, FF_BLK), lambda i, j, f=blk(k): (0, f(i, j))) for k in range(FF_SUB)]
    rows = [pl.BlockSpec((FF_BLK, d), lambda i, j, f=blk(k): (f(i, j), 0)) for k in range(FF_SUB)]
    return pl.pallas_call(
        _ffn_body,
        grid=(m // tm, pl.cdiv(D_FF, FF_TILE)),
        in_specs=[pl.BlockSpec((tm, d), lambda i, j: (i, 0)),
                  pl.BlockSpec((1, d), lambda i, j: (0, 0))] + cols + cols + rows,
        out_specs=pl.BlockSpec((tm, d), lambda i, j: (i, 0)),
        out_shape=jax.ShapeDtypeStruct((m, d), F32),
        scratch_shapes=[pltpu.VMEM((tm, d), BF16)],
        compiler_params=_params(("parallel", "arbitrary"), FFN_VMEM_LIMIT),
        name="ffn",
    )(x, gain, *([wg] * FF_SUB), *([wu] * FF_SUB), *([wd] * FF_SUB))


def _inproj_body(x_ref, g_ref, wf_ref, wb_ref, sc_ref, wgc_ref,
                 pf_ref, pb_ref, gc_ref, gr_ref, h_ref):
    j = pl.program_id(1)

    @pl.when(j == 0)
    def _():
        h = _rms(x_ref[...], g_ref[...]).astype(BF16)
        h_ref[...] = h
        gates = _dot(h, wgc_ref[...])
        gc_ref[...] = gates
        gr_ref[...] = gates.T[:GATE_ROWS, :]

    h = h_ref[...]
    pf_ref[...] = _dot(h, wf_ref[...])
    pb_ref[...] = (_dot(h, wb_ref[...]) * sc_ref[...]).astype(BF16)


def _inproj(x, gain, w_main, col_scale, w_gate_c, tm=1024):
    m, d = x.shape
    tn = PROJ_TILE
    return pl.pallas_call(
        _inproj_body,
        grid=(m // tm, D_HALF // tn),
        in_specs=[
            pl.BlockSpec((tm, d), lambda i, j: (i, 0)),
            pl.BlockSpec((1, d), lambda i, j: (0, 0)),
            pl.BlockSpec((d, tn), lambda i, j: (0, j)),
            pl.BlockSpec((d, tn), lambda i, j: (0, j + D_HALF // tn)),
            pl.BlockSpec((1, tn), lambda i, j: (0, j)),
            pl.BlockSpec((d, GATE_LANES), lambda i, j: (0, 0)),
        ],
        out_specs=[
            pl.BlockSpec((tm, tn), lambda i, j: (i, j)),
            pl.BlockSpec((tm, tn), lambda i, j: (i, j)),
            pl.BlockSpec((tm, GATE_LANES), lambda i, j: (i, 0)),
            pl.BlockSpec((GATE_ROWS, tm), lambda i, j: (0, i)),
        ],
        out_shape=[
            jax.ShapeDtypeStruct((m, D_HALF), F32),
            jax.ShapeDtypeStruct((m, D_HALF), BF16),
            jax.ShapeDtypeStruct((m, GATE_LANES), F32),
            jax.ShapeDtypeStruct((GATE_ROWS, m), F32),
        ],
        scratch_shapes=[pltpu.VMEM((tm, d), BF16)],
        compiler_params=_params(("parallel", "arbitrary")),
        name="inproj",
    )(x, gain, w_main, w_main, col_scale, w_gate_c)


def _lru_body(xr_ref, gate_ref, cw_ref, cb_ref, wa_ref, ba_ref, wx_ref, bx_ref, lam_ref,
              gain_ref, o_ref, xbuf, hcar):
    t = pl.program_id(1)
    tt = xr_ref.shape[1]
    width = xr_ref.shape[2]
    pad = 8

    @pl.when(t == 0)
    def _():
        xbuf[0:pad, :] = jnp.zeros((pad, width), F32)
        hcar[...] = jnp.zeros_like(hcar)

    xr = xr_ref[0]
    xbuf[pad:pad + tt, :] = xr
    xc = cb_ref[...]
    for j in range(LRU_CONV_WIDTH):
        off = pad - (LRU_CONV_WIDTH - 1) + j
        xc = xc + cw_ref[j:j + 1, :] * xbuf[pl.ds(off, tt), :]
    xbuf[0:pad, :] = xr[tt - pad:tt, :]

    xcb = xc.astype(BF16)
    ra, rx = [], []
    for n in range(N_HEADS):
        blk = xcb[:, n * HEAD_DIM:(n + 1) * HEAD_DIM]
        ra.append(_dot(blk, wa_ref[n]))
        rx.append(_dot(blk, wx_ref[n]))
    r = jax.nn.sigmoid(jnp.concatenate(ra, axis=1) + ba_ref[...])
    i = jax.nn.sigmoid(jnp.concatenate(rx, axis=1) + bx_ref[...])
    log_a = (-LRU_C * _softplus(-lam_ref[...])) * r
    a = jnp.exp(log_a)
    u = jnp.sqrt(-jnp.tanh(log_a) * (a * a + 1.0)) * (i * xc)

    sub = lax.broadcasted_iota(jnp.int32, (tt, width), 0) % SUBLANES
    s = 1
    while s < SUBLANES:
        keep = sub >= s
        a_sh = jnp.where(keep, pltpu.roll(a, s, 0), 1.0)
        u_sh = jnp.where(keep, pltpu.roll(u, s, 0), 0.0)
        u = a * u_sh + u
        a = a * a_sh
        s *= 2
    carry = hcar[...]
    groups = []
    for g in range(tt // SUBLANES):
        rows = slice(g * SUBLANES, (g + 1) * SUBLANES)
        hg = u[rows, :] + a[rows, :] * carry
        groups.append(hg)
        carry = hg[SUBLANES - 1:SUBLANES, :]
    h = jnp.concatenate(groups, axis=0)
    hcar[...] = carry

    y = h * jax.nn.gelu(gate_ref[0])
    o_ref[0] = _rms(y, gain_ref[...]).astype(BF16)


def _lru(pf3, conv_w, conv_b, w_a, b_a, w_x, b_x, lam, gain, tt=512):
    b, s, _ = pf3.shape
    w = GROUP_WIDTH
    vec = pl.BlockSpec((1, w), lambda bi, ti: (0, 0))
    mat = pl.BlockSpec((N_HEADS, HEAD_DIM, HEAD_DIM), lambda bi, ti: (0, 0, 0))
    return pl.pallas_call(
        _lru_body,
        grid=(b, s // tt),
        in_specs=[
            pl.BlockSpec((1, tt, w), lambda bi, ti: (bi, ti, PF_LX)),
            pl.BlockSpec((1, tt, w), lambda bi, ti: (bi, ti, PF_LG)),
            pl.BlockSpec((LRU_CONV_WIDTH, w), lambda bi, ti: (0, 0)),
            vec, mat, vec, mat, vec, vec, vec,
        ],
        out_specs=pl.BlockSpec((1, tt, w), lambda bi, ti: (bi, ti, 0)),
        out_shape=jax.ShapeDtypeStruct((b, s, w), BF16),
        scratch_shapes=[pltpu.VMEM((tt + 8, w), F32), pltpu.VMEM((1, w), F32)],
        compiler_params=_params(("parallel", "arbitrary")),
        name="rglru",
    )(pf3, pf3, conv_w, conv_b, w_a, b_a, w_x, b_x, lam, gain)


def _mlstm_body(q_ref, k_ref, v_ref, og_ref, gc_ref, gr_ref, bc_ref, br_ref, hg_ref, *rest):
    n_cast = (len(rest) - 3) // 2
    cast_src, out_ref, cast_dst = rest[:n_cast], rest[n_cast], rest[n_cast + 1:2 * n_cast + 1]
    c_ref, m_ref = rest[-2:]
    for src, dst in zip(cast_src, cast_dst):
        dst[...] = src[...].astype(BF16)
    c = pl.program_id(1)
    ln = q_ref.shape[1]

    @pl.when(c == 0)
    def _():
        c_ref[...] = jnp.zeros_like(c_ref)
        m_ref[...] = jnp.zeros_like(m_ref)

    gcol = gc_ref[0] + bc_ref[...]
    grow = gr_ref[...] + br_ref[...]
    lf_col = _log_sigmoid(gcol)
    lf_row = _log_sigmoid(grow)
    ri = lax.broadcasted_iota(jnp.int32, (ln, ln), 0)
    ci = lax.broadcasted_iota(jnp.int32, (ln, ln), 1)
    causal = ri >= ci
    tri_l = jnp.where(causal, 1.0, 0.0).astype(BF16)
    tri_u = jnp.where(ri <= ci, 1.0, 0.0).astype(BF16)
    b_col = sum(_dot(tri_l, part) for part in _split_bf16(lf_col, 3))
    b_row = sum(_dot(part, tri_u) for part in _split_bf16(lf_row, 3))

    ones_blk = jnp.ones((ln, HEAD_DIM), BF16)
    reps = ln // HEAD_DIM

    def wide(x):
        return jnp.concatenate([x] * reps, axis=1)

    heads = [slice(h * HEAD_DIM, (h + 1) * HEAD_DIM) for h in range(N_HEADS)]
    qs = [q_ref[0, :, sl] for sl in heads]
    ks = [k_ref[0, :, sl] for sl in heads]
    v_augs = [jnp.concatenate([v_ref[0, :, sl], ones_blk], axis=1) for sl in heads]
    c_augs = [c_ref[h] for h in range(N_HEADS)]

    decay_w, m_ts, w_inters, m_nexts, decays, kws = [], [], [], [], [], []
    for h in range(N_HEADS):
        bc = jnp.broadcast_to(b_col[:, N_HEADS + h:N_HEADS + h + 1], (ln, HEAD_DIM))
        ig = jnp.broadcast_to(gcol[:, h:h + 1], (ln, HEAD_DIM))
        brow = b_row[N_HEADS + h:N_HEADS + h + 1, :]
        igr = grow[h:h + 1, :]
        m_run = m_ref[h:h + 1, :]
        b_last = bc[ln - 1:ln, :]
        log_d = jnp.where(causal, wide(bc) - brow + igr, NEG_BIG)
        inter = bc + m_run
        m_t = jnp.maximum(inter, jnp.max(log_d, axis=1, keepdims=True))
        decay_w.append(jnp.exp(log_d - wide(m_t)))
        m_ts.append(m_t)
        w_inters.append(jnp.exp(inter - m_t))
        log_w = b_last - bc + ig
        m_next = jnp.maximum(b_last + m_run, jnp.max(log_w, axis=0, keepdims=True))
        m_nexts.append(m_next)
        decays.append(jnp.exp(b_last + m_run - m_next))
        kws.append((ks[h].astype(F32) * jnp.exp(log_w - m_next)).astype(BF16))

    qk = [_dot_nt(q, k) for q, k in zip(qs, ks)]
    qc = [_dot(q, c_aug.astype(BF16)) for q, c_aug in zip(qs, c_augs)]
    smats = [(s * d).astype(BF16) for s, d in zip(qk, decay_w)]
    intra = [_dot(s, v_aug) for s, v_aug in zip(smats, v_augs)]
    upd = [lax.dot_general(kw, v_aug, _TN, preferred_element_type=F32) for kw, v_aug in zip(kws, v_augs)]

    for h, sl in enumerate(heads):
        num = intra[h][:, :HEAD_DIM] + w_inters[h] * qc[h][:, :HEAD_DIM]
        den = intra[h][:, HEAD_DIM:] + w_inters[h] * qc[h][:, HEAD_DIM:]
        hh = num / jnp.maximum(jnp.abs(den), jnp.exp(-m_ts[h]))
        c_ref[h] = jnp.concatenate([decays[h]] * 2, axis=1) * c_augs[h] + upd[h]
        m_ref[h:h + 1, :] = m_nexts[h]
        hn = _rms(hh, hg_ref[:, sl])
        out_ref[0, :, sl] = (hn * jax.nn.sigmoid(og_ref[0, :, sl])).astype(BF16)


def _cast_rows(rows, steps):
    rb = -(-rows // steps)
    rb = -(-rb // BF16_ROWS) * BF16_ROWS
    while rows % rb:
        rb += BF16_ROWS
    return rb


def _mlstm(pf3, pb3, gates_c, gates_r, bias_c, bias_r, head_gain, cast_jobs=(), ln=256):
    b, s, _ = pf3.shape
    nc = s // ln
    w = GROUP_WIDTH

    def col(g):
        return pl.BlockSpec((1, ln, w), lambda bi, ci, g=g: (bi, ci, g))

    cast_in, cast_out, cast_shapes = [], [], []
    for arr, layer in cast_jobs:
        _, rows, cols = arr.shape
        rb = _cast_rows(rows, b * nc)
        step = lambda bi, ci, last=rows // rb - 1: jnp.minimum(bi * nc + ci, last)
        cast_in.append(pl.BlockSpec((None, rb, cols), lambda bi, ci, f=step, l=layer: (l, f(bi, ci), 0)))
        cast_out.append(pl.BlockSpec((rb, cols), lambda bi, ci, f=step: (f(bi, ci), 0)))
        cast_shapes.append(jax.ShapeDtypeStruct((rows, cols), BF16))

    out = pl.pallas_call(
        _mlstm_body,
        grid=(b, nc),
        in_specs=[
            col(PB_MQ), col(PB_MK), col(PB_MV), col(PF_MO),
            pl.BlockSpec((1, ln, GATE_LANES), lambda bi, ci: (bi, ci, 0)),
            pl.BlockSpec((GATE_ROWS, ln), lambda bi, ci: (0, bi * nc + ci)),
            pl.BlockSpec((1, GATE_LANES), lambda bi, ci: (0, 0)),
            pl.BlockSpec((GATE_ROWS, 1), lambda bi, ci: (0, 0)),
            pl.BlockSpec((1, w), lambda bi, ci: (0, 0)),
        ] + cast_in,
        out_specs=[pl.BlockSpec((1, ln, w), lambda bi, ci: (bi, ci, 0))] + cast_out,
        out_shape=[jax.ShapeDtypeStruct((b, s, w), BF16)] + cast_shapes,
        scratch_shapes=[pltpu.VMEM((N_HEADS, HEAD_DIM, 2 * HEAD_DIM), F32),
                        pltpu.VMEM((8, 128), F32)],
        compiler_params=_params(("arbitrary", "arbitrary")),
        name="mlstm",
    )(pb3, pb3, pb3, pf3, gates_c.reshape(b, s, GATE_LANES), gates_r, bias_c, bias_r, head_gain,
      *[arr for arr, _ in cast_jobs])
    return out[0], tuple(out[1:])


def _dil_body(q_ref, k_ref, v_ref, qg_ref, kg_ref, sl_ref, o_ref, qn, kn, m_s, l_s):
    s = q_ref.shape[1]
    n_blocks = s // BLK
    vv, o2 = v_ref.at[0], o_ref.at[0]
    qn[...] = _rms(q_ref[0], qg_ref[...]) * ATTN_SCALE
    kn[...] = _rms(k_ref[0], kg_ref[...])
    slope = sl_ref[0, 0:1, 0:1]
    qq = lax.broadcasted_iota(jnp.int32, (BLK, 2 * BLK), 0)
    kk = lax.broadcasted_iota(jnp.int32, (BLK, 2 * BLK), 1)
    dist = jnp.where(kk < BLK, qq - kk, qq - kk + 2 * BLK)
    in_window = jnp.logical_and(dist >= 0, dist <= BLK)
    prev_lanes = lax.broadcasted_iota(jnp.int32, (1, 2 * BLK), 1) >= BLK
    ones = jnp.ones((2 * BLK, HEAD_DIM), BF16)
    assert n_blocks % DIL_PAR == 0

    for pi, (window, dil) in enumerate(reversed(DILATED_PATTERNS)):
        assert window // dil == BLK and s % (dil * BLK) == 0
        nb = s // (dil * BLK)
        assert nb % DIL_PAR == 0 or DIL_PAR % nb == 0
        bias = jnp.where(in_window, (-float(dil) * slope) * dist.astype(F32), NEG_BIG)
        first, last = pi == 0, pi == len(DILATED_PATTERNS) - 1

        def rows(start, dil=dil):
            return pl.ds(start, BLK) if dil == 1 else pl.ds(start, BLK, stride=dil)

        def group(t0, carry, dil=dil, nb=nb, bias=bias, first=first, last=last, rows=rows):
            cur, qb, kc, vc, no_prev = [], [], [], [], []
            for i in range(DIL_PAR):
                t = t0 * DIL_PAR + i
                r = t // nb
                n = t - r * nb
                c = rows(r + dil * BLK * n)
                cur.append(c)
                qb.append(qn[c, :].astype(BF16))
                kc.append(kn[c, :].astype(BF16))
                vc.append(vv[c, :].astype(BF16))
                no_prev.append(jnp.where(prev_lanes, jnp.where(n > 0, 0.0, NEG_BIG), 0.0))
                if i == 0:
                    p = rows(r + dil * BLK * jnp.maximum(n - 1, 0))
                    kp, vp = [kn[p, :].astype(BF16)], [vv[p, :].astype(BF16)]
                elif i % nb == 0:
                    kp.append(kc[i])
                    vp.append(vc[i])
                else:
                    kp.append(kc[i - 1])
                    vp.append(vc[i - 1])
            k2 = [jnp.concatenate([a, b], axis=0) for a, b in zip(kc, kp)]
            v2 = [jnp.concatenate([a, b], axis=0) for a, b in zip(vc, vp)]
            sc = [jnp.maximum(_dot_nt(q, k) + bias + off, NEG_BIG) for q, k, off in zip(qb, k2, no_prev)]
            m_b = [jnp.max(x, axis=1, keepdims=True) for x in sc]
            pr = [jnp.exp(x - m).astype(BF16) for x, m in zip(sc, m_b)]
            nd = [_dot(p, jnp.concatenate([v, ones], axis=1)) for p, v in zip(pr, v2)]
            res = []
            for c, m, x in zip(cur, m_b, nd):
                num, den = x[:, :HEAD_DIM], x[:, HEAD_DIM:]
                if first:
                    res.append((jnp.broadcast_to(m, (BLK, HEAD_DIM)), den, num))
                else:
                    m_o = m_s[c, :]
                    m_n = jnp.maximum(m_o, m)
                    a_o = jnp.exp(m_o - m_n)
                    a_b = jnp.exp(m - m_n)
                    res.append((m_n, l_s[c, :] * a_o + den * a_b, o2[c, :] * a_o + num * a_b))
            for c, (m_n, l_n, acc) in zip(cur, res):
                if last:
                    o2[c, :] = acc / l_n
                else:
                    m_s[c, :] = m_n
                    l_s[c, :] = l_n
                    o2[c, :] = acc
            return carry

        lax.fori_loop(0, n_blocks // DIL_PAR, group, 0)


def _dilated(pf3, q_gain, k_gain, slopes):
    b, s, _ = pf3.shape

    def col(g):
        return pl.BlockSpec((1, s, HEAD_DIM), lambda bi, hi, g=g: (bi, 0, g * N_HEADS + hi))

    vec = pl.BlockSpec((1, HEAD_DIM), lambda bi, hi: (0, 0))
    return pl.pallas_call(
        _dil_body,
        grid=(b, N_HEADS),
        in_specs=[col(PF_CQ), col(PF_CK), col(PF_CV), vec, vec,
                  pl.BlockSpec((1, 8, HEAD_DIM), lambda bi, hi: (hi, 0, 0))],
        out_specs=pl.BlockSpec((1, s, HEAD_DIM), lambda bi, hi: (bi, 0, hi)),
        out_shape=jax.ShapeDtypeStruct((b, s, GROUP_WIDTH), F32),
        scratch_shapes=[pltpu.VMEM((s, HEAD_DIM), F32)] * 4,
        compiler_params=_params(("parallel", "parallel")),
        name="dilated",
    )(pf3, pf3, pf3, q_gain, k_gain, slopes)


def _sb_body(q_ref, k_ref, v_ref, o_ref, z_ref):
    i = pl.program_id(2)
    nr, nk = SB_Q, SB_K
    per_tile = nr // nk
    qq = lax.broadcasted_iota(jnp.int32, (nr, nk), 0)
    kk = lax.broadcasted_iota(jnp.int32, (nr, nk), 1)
    ka = lax.broadcasted_iota(jnp.int32, (nk, nk), 0)
    kb = lax.broadcasted_iota(jnp.int32, (nk, nk), 1)
    after = jnp.where(ka > kb, 1.0, 0.0).astype(BF16)
    o_ref[...] = jnp.zeros_like(o_ref)

    heads = [slice(h * HEAD_DIM, (h + 1) * HEAD_DIM) for h in range(SB_HEADS)]

    def logits(j):
        keys = pl.ds(pl.multiple_of(j * nk, nk), nk)
        return [_dot_nt(q_ref[0, :, sl], k_ref[0, keys, sl]) for sl in heads]

    def step(j, gone, masked):
        keys = pl.ds(pl.multiple_of(j * nk, nk), nk)
        strict = kk < qq if per_tile == 1 else (kk + j * nk) < (qq + i * nr)
        zs = [z_ref[h] for h in range(SB_HEADS)]
        sps = [jnp.maximum(z, 0.0) + jnp.log2(1.0 + jnp.exp2(-jnp.abs(z))) for z in zs]
        log_beta = [z - sp for z, sp in zip(zs, sps)]
        drops = [jnp.where(strict, sp, 0.0) for sp in sps] if masked else sps
        gone_next = tuple(g + jnp.sum(drop, axis=1, keepdims=True) for g, drop in zip(gone, drops))
        for h, z in enumerate(logits(jnp.maximum(j - 1, 0))):
            z_ref[h] = z
        laters = [_dot(drop.astype(BF16), after) for drop in drops]
        ws = [jnp.exp2(lb - later - g) for lb, later, g in zip(log_beta, laters, gone)]
        if masked:
            ws = [jnp.where(strict, w, 0.0) for w in ws]
        pv = [_dot(w.astype(BF16), v_ref[0, keys, sl]) for w, sl in zip(ws, heads)]
        o_ref[0] += jnp.concatenate(pv, axis=1)
        return gone_next

    top = per_tile * i + per_tile - 1
    for h, z in enumerate(logits(top)):
        z_ref[h] = z
    gone = tuple(jnp.zeros((nr, 1), F32) for _ in range(SB_HEADS))
    for r in range(per_tile):
        gone = step(top - r, gone, True)
    lax.fori_loop(0, per_tile * i, lambda jj, g: step(per_tile * i - 1 - jj, g, False), gone)


def _stick_breaking(pb3):
    b, s, _ = pb3.shape
    w = SB_HEADS * HEAD_DIM
    groups = N_HEADS // SB_HEADS
    return pl.pallas_call(
        _sb_body,
        grid=(b, groups, s // SB_Q),
        in_specs=[
            pl.BlockSpec((1, SB_Q, w), lambda bi, gi, qi: (bi, qi, PB_SQ * groups + gi)),
            pl.BlockSpec((1, s, w), lambda bi, gi, qi: (bi, 0, PB_SK * groups + gi)),
            pl.BlockSpec((1, s, w), lambda bi, gi, qi: (bi, 0, PB_SV * groups + gi)),
        ],
        out_specs=pl.BlockSpec((1, SB_Q, w), lambda bi, gi, qi: (bi, qi, gi)),
        out_shape=jax.ShapeDtypeStruct((b, s, GROUP_WIDTH), F32),
        scratch_shapes=[pltpu.VMEM((SB_HEADS, SB_Q, SB_K), F32)],
        compiler_params=_params(("parallel", "parallel", "parallel")),
        name="stick_breaking",
    )(pb3, pb3, pb3)


def _cast_body(src, dst):
    dst[...] = src[...].astype(BF16)


def _cast_layer(stacked, layer):
    _, rows, cols = stacked.shape
    rb = _cast_rows(rows, CAST_STEPS)
    return pl.pallas_call(
        _cast_body,
        grid=(rows // rb,),
        in_specs=[pl.BlockSpec((None, rb, cols), lambda i: (layer, i, 0))],
        out_specs=pl.BlockSpec((rb, cols), lambda i: (i, 0)),
        out_shape=jax.ShapeDtypeStruct((rows, cols), BF16),
        compiler_params=_params(("parallel",)),
        name="cast_bf16",
    )(stacked)


def _outproj_body(x_ref, ya_ref, yb_ref, yc_ref, yd_ref, gc_ref, gd_ref, w_ref, o_ref):
    yc = _rms(yc_ref[...], gc_ref[...]).astype(BF16)
    yd = _rms(yd_ref[...], gd_ref[...]).astype(BF16)
    w = GROUP_WIDTH
    acc = _dot(ya_ref[...], w_ref[0:w, :])
    acc += _dot(yb_ref[...], w_ref[w:2 * w, :])
    acc += _dot(yc, w_ref[2 * w:3 * w, :])
    acc += _dot(yd, w_ref[3 * w:4 * w, :])
    o_ref[...] = x_ref[...] + acc


def _outproj(x, ya, yb, yc, yd, gain_c, gain_d, w_out, tm=512):
    m, d = x.shape
    w = GROUP_WIDTH
    yblk = pl.BlockSpec((tm, w), lambda i: (i, 0))
    vec = pl.BlockSpec((1, w), lambda i: (0, 0))
    return pl.pallas_call(
        _outproj_body,
        grid=(m // tm,),
        in_specs=[pl.BlockSpec((tm, d), lambda i: (i, 0)), yblk, yblk, yblk, yblk, vec, vec,
                  pl.BlockSpec((4 * w, d), lambda i: (0, 0))],
        out_specs=pl.BlockSpec((tm, d), lambda i: (i, 0)),
        out_shape=jax.ShapeDtypeStruct((m, d), F32),
        compiler_params=_params(("parallel",)),
        name="outproj",
    )(x, ya, yb, yc, yd, gain_c, gain_d, w_out)


def _row(v):
    return v.reshape(1, -1).astype(F32)


_W_IN_GROUP_ROWS = (0, 512, 2560, 3080, 3592, 4104, 1024, 1536, 2048, 4616, 5128, 5640)
_W_IN_GATE_ROW = 6 * GROUP_WIDTH


def _w_in_group_row(g):
    row = jnp.int32(_W_IN_GROUP_ROWS[0])
    for k in range(1, len(_W_IN_GROUP_ROWS)):
        row = jnp.where(g == k, _W_IN_GROUP_ROWS[k], row)
    return row


def _split_w_in_body(grp_ref, gate_ref, w_ref, wg_ref):
    w_ref[...] = grp_ref[...].T.astype(BF16)

    @pl.when(pl.program_id(0) == 0)
    def _():
        gates = gate_ref[...]
        pad = jnp.zeros((GATE_LANES - gates.shape[0], gates.shape[1]), F32)
        wg_ref[...] = jnp.concatenate([gates, pad], axis=0).T.astype(BF16)


def _split_w_in(w_in_t, layer):
    n_layers, d_in, d = w_in_t.shape
    n_groups = len(_W_IN_GROUP_ROWS)
    base = layer * d_in
    assert base % SUBLANES == 0 and all(r % SUBLANES == 0 for r in _W_IN_GROUP_ROWS)
    flat = w_in_t.reshape(n_layers * d_in, d)
    return pl.pallas_call(
        _split_w_in_body,
        grid=(n_groups,),
        in_specs=[pl.BlockSpec((pl.Element(GROUP_WIDTH), pl.Element(d)),
                               lambda g: (pl.multiple_of(base + _w_in_group_row(g), SUBLANES), 0)),
                  pl.BlockSpec((pl.Element(2 * N_HEADS), pl.Element(d)),
                               lambda g: (base + _W_IN_GATE_ROW, 0))],
        out_specs=[pl.BlockSpec((d, GROUP_WIDTH), lambda g: (0, g)),
                   pl.BlockSpec((d, GATE_LANES), lambda g: (0, 0))],
        out_shape=[jax.ShapeDtypeStruct((d, n_groups * GROUP_WIDTH), BF16),
                   jax.ShapeDtypeStruct((d, GATE_LANES), BF16)],
        compiler_params=_params(("arbitrary",)),
        name="split_w_in",
    )(flat, flat)


def kernel(x, ffn1_norm, ffn1_w_gate, ffn1_w_up, ffn1_w_down, mix_norm, w_in, lru_conv_w, lru_conv_b, lru_w_a, lru_b_a, lru_w_x, lru_b_x, lru_lambda, mlstm_ig_bias, mlstm_fg_bias, attn_q_gain, attn_k_gain, group_out_gain, w_out, ffn2_norm, ffn2_w_gate, ffn2_w_up, ffn2_w_down):
    b, s, d = x.shape
    depth = w_in.shape[0]
    m = b * s
    w = GROUP_WIDTH
    slopes = 2.0 ** (-8.0 * jnp.arange(1, N_HEADS + 1, dtype=F32) / N_HEADS)
    slopes = jnp.broadcast_to(slopes[:, None, None], (N_HEADS, 8, HEAD_DIM))
    col_scale = jnp.ones((6, w), F32).at[PB_MK].set(ATTN_SCALE).at[PB_SQ].set(SB_Q_SCALE).reshape(1, D_HALF)

    ffn_w = tuple(_cast_layer(t, 0) for t in (ffn1_w_gate, ffn1_w_up, ffn1_w_down))

    w_in_t = jnp.swapaxes(w_in, 1, 2)

    xf = x.reshape(m, d)
    for l in range(depth):
        xf = _ffn(xf, _row(ffn1_norm[l]), ffn_w)

        w_main, w_gate = _split_w_in(w_in_t, l)
        pf, pb, gates_c, gates_r = _inproj(xf, _row(mix_norm[l]), w_main, col_scale, w_gate)
        pf3 = pf.reshape(b, s, D_HALF)
        pb3 = pb.reshape(b, s, D_HALF)

        gains = group_out_gain[l].reshape(4, 1, w)
        ya = _lru(pf3, lru_conv_w[l], _row(lru_conv_b[l]), lru_w_a[l].astype(BF16), _row(lru_b_a[l]),
                  lru_w_x[l].astype(BF16), _row(lru_b_x[l]), _row(lru_lambda[l]), gains[0])

        gate_bias = jnp.concatenate([mlstm_ig_bias[l], mlstm_fg_bias[l]]).astype(F32)
        bias_c = jnp.pad(gate_bias, (0, GATE_LANES - 2 * N_HEADS)).reshape(1, GATE_LANES)
        bias_r = jnp.pad(gate_bias, (0, GATE_ROWS - 2 * N_HEADS)).reshape(GATE_ROWS, 1)
        jobs = [(w_out, l)] + [(t, l) for t in (ffn2_w_gate, ffn2_w_up, ffn2_w_down)]
        if l + 1 < depth:
            jobs += [(t, l + 1) for t in (ffn1_w_gate, ffn1_w_up, ffn1_w_down)]
        yb, cast = _mlstm(pf3, pb3, gates_c, gates_r, bias_c, bias_r, gains[1], jobs)
        w_out_b, ffn2_w, ffn_w = cast[0], cast[1:4], cast[4:]

        yc = _dilated(pf3, _row(attn_q_gain[l]), _row(attn_k_gain[l]), slopes)
        yd = _stick_breaking(pb3)

        xf = _outproj(xf, ya.reshape(m, w), yb.reshape(m, w), yc.reshape(m, w), yd.reshape(m, w),
                      gains[2], gains[3], w_out_b)

        xf = _ffn(xf, _row(ffn2_norm[l]), ffn2_w)
    return xf.reshape(b, s, d)
```

```python
import jax
import jax.numpy as jnp
from jax import lax
from jax.experimental import pallas as pl
from jax.experimental.pallas import tpu as pltpu

F32 = jnp.float32
BF16 = jnp.bfloat16

D_MODEL = 2048
N_HEADS = 4
HEAD_DIM = 128
GROUP_WIDTH = 512
D_FF = 5504
FF_BLK = 128
FF_SUB = 4
FF_TILE = FF_SUB * FF_BLK
LRU_C = 8.0
LRU_CONV_WIDTH = 4
DILATED_PATTERNS = ((128, 1), (512, 4), (2048, 16))
BLK = 128
SB_Q = 256
SB_K = 256
SB_HEADS = 4
DIL_PAR = 8
RMS_EPS = 1e-6
NEG_BIG = -1e30
ATTN_SCALE = HEAD_DIM ** -0.5
SB_Q_SCALE = ATTN_SCALE * 1.4426950408889634
GATE_LANES = 128
GATE_ROWS = 16
SUBLANES = 8
BF16_ROWS = 16
CAST_STEPS = 8
VMEM_LIMIT = 52 * 1024 * 1024
FFN_VMEM_LIMIT = 58 * 1024 * 1024

PF_LX, PF_LG, PF_MO, PF_CQ, PF_CK, PF_CV = range(6)
PB_MQ, PB_MK, PB_MV, PB_SQ, PB_SK, PB_SV = range(6)
D_HALF = 6 * GROUP_WIDTH
PROJ_TILE = 512

_NT = (((1,), (1,)), ((), ()))
_TN = (((0,), (0,)), ((), ()))


def _rms(x, gain):
    return x * lax.rsqrt(jnp.mean(x * x, axis=-1, keepdims=True) + RMS_EPS) * gain


def _softplus(x):
    return jnp.maximum(x, 0.0) + jnp.log(1.0 + jnp.exp(-jnp.abs(x)))


def _log_sigmoid(x):
    return -_softplus(-x)


def _dot(a, b):
    return jnp.dot(a, b, preferred_element_type=F32)


def _dot_nt(a, b):
    return lax.dot_general(a, b, _NT, preferred_element_type=F32)


def _split_bf16(x, parts):
    out = []
    r = x
    for _ in range(parts):
        t = r.astype(BF16)
        out.append(t)
        r = r - t.astype(F32)
    return out


def _params(sem, vmem_limit=None):
    return pltpu.CompilerParams(dimension_semantics=sem, vmem_limit_bytes=vmem_limit or VMEM_LIMIT)


def _ffn_body(x_ref, g_ref, *refs):
    wg_refs, wu_refs, wd_refs = refs[:FF_SUB], refs[FF_SUB:2 * FF_SUB], refs[2 * FF_SUB:3 * FF_SUB]
    o_ref, h_ref = refs[3 * FF_SUB:]
    j = pl.program_id(1)

    @pl.when(j == 0)
    def _():
        h_ref[...] = _rms(x_ref[...], g_ref[...]).astype(BF16)
        o_ref[...] = jnp.zeros_like(o_ref)

    h = h_ref[...]
    g = _dot(h, jnp.concatenate([r[...] for r in wg_refs], axis=1))
    u = _dot(h, jnp.concatenate([r[...] for r in wu_refs], axis=1))
    col = lax.broadcasted_iota(jnp.int32, (1, FF_TILE), 1)
    a = jnp.where(col < D_FF - j * FF_TILE, (g * jax.nn.sigmoid(g)) * u, 0.0)
    o_ref[...] += _dot(a.astype(BF16), jnp.concatenate([r[...] for r in wd_refs], axis=0))

    @pl.when(j == pl.num_programs(1) - 1)
    def _():
        o_ref[...] = x_ref[...] + 0.5 * o_ref[...]


def _ffn(x, gain, weights, tm=1024):
    wg, wu, wd = weights
    m, d = x.shape
    last_blk = D_FF // FF_BLK - 1

    def blk(k):
        return lambda i, j: jnp.minimum(j * FF_SUB + k, last_blk)

    cols = [pl.BlockSpec((d, FF_BLK), lambda i, j, f=blk(k): (0, f(i, j))) for k in range(FF_SUB)]
    rows = [pl.BlockSpec((FF_BLK, d), lambda i, j, f=blk(k): (f(i, j), 0)) for k in range(FF_SUB)]
    return pl.pallas_call(
        _ffn_body,
        grid=(m // tm, pl.cdiv(D_FF, FF_TILE)),
        in_specs=[pl.BlockSpec((tm, d), lambda i, j: (i, 0)),
                  pl.BlockSpec((1, d), lambda i, j: (0, 0))] + cols + cols + rows,
        out_specs=pl.BlockSpec((tm, d), lambda i, j: (i, 0)),
        out_shape=jax.ShapeDtypeStruct((m, d), F32),
        scratch_shapes=[pltpu.VMEM((tm, d), BF16)],
        compiler_params=_params(("parallel", "arbitrary"), FFN_VMEM_LIMIT),
        name="ffn",
    )(x, gain, *([wg] * FF_SUB), *([wu] * FF_SUB), *([wd] * FF_SUB))


def _inproj_body(x_ref, g_ref, wf_ref, wb_ref, sc_ref, wgc_ref,
                 pf_ref, pb_ref, gc_ref, gr_ref, h_ref):
    j = pl.program_id(1)

    @pl.when(j == 0)
    def _():
        h = _rms(x_ref[...], g_ref[...]).astype(BF16)
        h_ref[...] = h
        gates = _dot(h, wgc_ref[...])
        gc_ref[...] = gates
        gr_ref[...] = gates.T[:GATE_ROWS, :]

    h = h_ref[...]
    pf_ref[...] = _dot(h, wf_ref[...])
    pb_ref[...] = (_dot(h, wb_ref[...]) * sc_ref[...]).astype(BF16)


def _inproj(x, gain, w_main, col_scale, w_gate_c, tm=1024):
    m, d = x.shape
    tn = PROJ_TILE
    return pl.pallas_call(
        _inproj_body,
        grid=(m // tm, D_HALF // tn),
        in_specs=[
            pl.BlockSpec((tm, d), lambda i, j: (i, 0)),
            pl.BlockSpec((1, d), lambda i, j: (0, 0)),
            pl.BlockSpec((d, tn), lambda i, j: (0, j)),
            pl.BlockSpec((d, tn), lambda i, j: (0, j + D_HALF // tn)),
            pl.BlockSpec((1, tn), lambda i, j: (0, j)),
            pl.BlockSpec((d, GATE_LANES), lambda i, j: (0, 0)),
        ],
        out_specs=[
            pl.BlockSpec((tm, tn), lambda i, j: (i, j)),
            pl.BlockSpec((tm, tn), lambda i, j: (i, j)),
            pl.BlockSpec((tm, GATE_LANES), lambda i, j: (i, 0)),
            pl.BlockSpec((GATE_ROWS, tm), lambda i, j: (0, i)),
        ],
        out_shape=[
            jax.ShapeDtypeStruct((m, D_HALF), F32),
            jax.ShapeDtypeStruct((m, D_HALF), BF16),
            jax.ShapeDtypeStruct((m, GATE_LANES), F32),
            jax.ShapeDtypeStruct((GATE_ROWS, m), F32),
        ],
        scratch_shapes=[pltpu.VMEM((tm, d), BF16)],
        compiler_params=_params(("parallel", "arbitrary")),
        name="inproj",
    )(x, gain, w_main, w_main, col_scale, w_gate_c)


def _lru_body(xr_ref, gate_ref, cw_ref, cb_ref, wa_ref, ba_ref, wx_ref, bx_ref, lam_ref,
              gain_ref, o_ref, xbuf, hcar):
    t = pl.program_id(1)
    tt = xr_ref.shape[1]
    width = xr_ref.shape[2]
    pad = 8

    @pl.when(t == 0)
    def _():
        xbuf[0:pad, :] = jnp.zeros((pad, width), F32)
        hcar[...] = jnp.zeros_like(hcar)

    xr = xr_ref[0]
    xbuf[pad:pad + tt, :] = xr
    xc = cb_ref[...]
    for j in range(LRU_CONV_WIDTH):
        off = pad - (LRU_CONV_WIDTH - 1) + j
        xc = xc + cw_ref[j:j + 1, :] * xbuf[pl.ds(off, tt), :]
    xbuf[0:pad, :] = xr[tt - pad:tt, :]

    xcb = xc.astype(BF16)
    ra, rx = [], []
    for n in range(N_HEADS):
        blk = xcb[:, n * HEAD_DIM:(n + 1) * HEAD_DIM]
        ra.append(_dot(blk, wa_ref[n]))
        rx.append(_dot(blk, wx_ref[n]))
    r = jax.nn.sigmoid(jnp.concatenate(ra, axis=1) + ba_ref[...])
    i = jax.nn.sigmoid(jnp.concatenate(rx, axis=1) + bx_ref[...])
    log_a = (-LRU_C * _softplus(-lam_ref[...])) * r
    a = jnp.exp(log_a)
    u = jnp.sqrt(-jnp.tanh(log_a) * (a * a + 1.0)) * (i * xc)

    sub = lax.broadcasted_iota(jnp.int32, (tt, width), 0) % SUBLANES
    s = 1
    while s < SUBLANES:
        keep = sub >= s
        a_sh = jnp.where(keep, pltpu.roll(a, s, 0), 1.0)
        u_sh = jnp.where(keep, pltpu.roll(u, s, 0), 0.0)
        u = a * u_sh + u
        a = a * a_sh
        s *= 2
    carry = hcar[...]
    groups = []
    for g in range(tt // SUBLANES):
        rows = slice(g * SUBLANES, (g + 1) * SUBLANES)
        hg = u[rows, :] + a[rows, :] * carry
        groups.append(hg)
        carry = hg[SUBLANES - 1:SUBLANES, :]
    h = jnp.concatenate(groups, axis=0)
    hcar[...] = carry

    y = h * jax.nn.gelu(gate_ref[0])
    o_ref[0] = _rms(y, gain_ref[...]).astype(BF16)


def _lru(pf3, conv_w, conv_b, w_a, b_a, w_x, b_x, lam, gain, tt=512):
    b, s, _ = pf3.shape
    w = GROUP_WIDTH
    vec = pl.BlockSpec((1, w), lambda bi, ti: (0, 0))
    mat = pl.BlockSpec((N_HEADS, HEAD_DIM, HEAD_DIM), lambda bi, ti: (0, 0, 0))
    return pl.pallas_call(
        _lru_body,
        grid=(b, s // tt),
        in_specs=[
            pl.BlockSpec((1, tt, w), lambda bi, ti: (bi, ti, PF_LX)),
            pl.BlockSpec((1, tt, w), lambda bi, ti: (bi, ti, PF_LG)),
            pl.BlockSpec((LRU_CONV_WIDTH, w), lambda bi, ti: (0, 0)),
            vec, mat, vec, mat, vec, vec, vec,
        ],
        out_specs=pl.BlockSpec((1, tt, w), lambda bi, ti: (bi, ti, 0)),
        out_shape=jax.ShapeDtypeStruct((b, s, w), BF16),
        scratch_shapes=[pltpu.VMEM((tt + 8, w), F32), pltpu.VMEM((1, w), F32)],
        compiler_params=_params(("parallel", "arbitrary")),
        name="rglru",
    )(pf3, pf3, conv_w, conv_b, w_a, b_a, w_x, b_x, lam, gain)


def _mlstm_body(q_ref, k_ref, v_ref, og_ref, gc_ref, gr_ref, bc_ref, br_ref, hg_ref, *rest):
    n_cast = (len(rest) - 3) // 2
    cast_src, out_ref, cast_dst = rest[:n_cast], rest[n_cast], rest[n_cast + 1:2 * n_cast + 1]
    c_ref, m_ref = rest[-2:]
    for src, dst in zip(cast_src, cast_dst):
        dst[...] = src[...].astype(BF16)
    c = pl.program_id(1)
    ln = q_ref.shape[1]

    @pl.when(c == 0)
    def _():
        c_ref[...] = jnp.zeros_like(c_ref)
        m_ref[...] = jnp.zeros_like(m_ref)

    gcol = gc_ref[0] + bc_ref[...]
    grow = gr_ref[...] + br_ref[...]
    lf_col = _log_sigmoid(gcol)
    lf_row = _log_sigmoid(grow)
    ri = lax.broadcasted_iota(jnp.int32, (ln, ln), 0)
    ci = lax.broadcasted_iota(jnp.int32, (ln, ln), 1)
    causal = ri >= ci
    tri_l = jnp.where(causal, 1.0, 0.0).astype(BF16)
    tri_u = jnp.where(ri <= ci, 1.0, 0.0).astype(BF16)
    b_col = sum(_dot(tri_l, part) for part in _split_bf16(lf_col, 3))
    b_row = sum(_dot(part, tri_u) for part in _split_bf16(lf_row, 3))

    ones_blk = jnp.ones((ln, HEAD_DIM), BF16)
    reps = ln // HEAD_DIM

    def wide(x):
        return jnp.concatenate([x] * reps, axis=1)

    heads = [slice(h * HEAD_DIM, (h + 1) * HEAD_DIM) for h in range(N_HEADS)]
    qs = [q_ref[0, :, sl] for sl in heads]
    ks = [k_ref[0, :, sl] for sl in heads]
    v_augs = [jnp.concatenate([v_ref[0, :, sl], ones_blk], axis=1) for sl in heads]
    c_augs = [c_ref[h] for h in range(N_HEADS)]

    decay_w, m_ts, w_inters, m_nexts, decays, kws = [], [], [], [], [], []
    for h in range(N_HEADS):
        bc = jnp.broadcast_to(b_col[:, N_HEADS + h:N_HEADS + h + 1], (ln, HEAD_DIM))
        ig = jnp.broadcast_to(gcol[:, h:h + 1], (ln, HEAD_DIM))
        brow = b_row[N_HEADS + h:N_HEADS + h + 1, :]
        igr = grow[h:h + 1, :]
        m_run = m_ref[h:h + 1, :]
        b_last = bc[ln - 1:ln, :]
        log_d = jnp.where(causal, wide(bc) - brow + igr, NEG_BIG)
        inter = bc + m_run
        m_t = jnp.maximum(inter, jnp.max(log_d, axis=1, keepdims=True))
        decay_w.append(jnp.exp(log_d - wide(m_t)))
        m_ts.append(m_t)
        w_inters.append(jnp.exp(inter - m_t))
        log_w = b_last - bc + ig
        m_next = jnp.maximum(b_last + m_run, jnp.max(log_w, axis=0, keepdims=True))
        m_nexts.append(m_next)
        decays.append(jnp.exp(b_last + m_run - m_next))
        kws.append((ks[h].astype(F32) * jnp.exp(log_w - m_next)).astype(BF16))

    qk = [_dot_nt(q, k) for q, k in zip(qs, ks)]
    qc = [_dot(q, c_aug.astype(BF16)) for q, c_aug in zip(qs, c_augs)]
    smats = [(s * d).astype(BF16) for s, d in zip(qk, decay_w)]
    intra = [_dot(s, v_aug) for s, v_aug in zip(smats, v_augs)]
    upd = [lax.dot_general(kw, v_aug, _TN, preferred_element_type=F32) for kw, v_aug in zip(kws, v_augs)]

    for h, sl in enumerate(heads):
        num = intra[h][:, :HEAD_DIM] + w_inters[h] * qc[h][:, :HEAD_DIM]
        den = intra[h][:, HEAD_DIM:] + w_inters[h] * qc[h][:, HEAD_DIM:]
        hh = num / jnp.maximum(jnp.abs(den), jnp.exp(-m_ts[h]))
        c_ref[h] = jnp.concatenate([decays[h]] * 2, axis=1) * c_augs[h] + upd[h]
        m_ref[h:h + 1, :] = m_nexts[h]
        hn = _rms(hh, hg_ref[:, sl])
        out_ref[0, :, sl] = (hn * jax.nn.sigmoid(og_ref[0, :, sl])).astype(BF16)


def _cast_rows(rows, steps):
    rb = -(-rows // steps)
    rb = -(-rb // BF16_ROWS) * BF16_ROWS
    while rows % rb:
        rb += BF16_ROWS
    return rb


def _mlstm(pf3, pb3, gates_c, gates_r, bias_c, bias_r, head_gain, cast_jobs=(), ln=256):
    b, s, _ = pf3.shape
    nc = s // ln
    w = GROUP_WIDTH

    def col(g):
        return pl.BlockSpec((1, ln, w), lambda bi, ci, g=g: (bi, ci, g))

    cast_in, cast_out, cast_shapes = [], [], []
    for arr, layer in cast_jobs:
        _, rows, cols = arr.shape
        rb = _cast_rows(rows, b * nc)
        step = lambda bi, ci, last=rows // rb - 1: jnp.minimum(bi * nc + ci, last)
        cast_in.append(pl.BlockSpec((None, rb, cols), lambda bi, ci, f=step, l=layer: (l, f(bi, ci), 0)))
        cast_out.append(pl.BlockSpec((rb, cols), lambda bi, ci, f=step: (f(bi, ci), 0)))
        cast_shapes.append(jax.ShapeDtypeStruct((rows, cols), BF16))

    out = pl.pallas_call(
        _mlstm_body,
        grid=(b, nc),
        in_specs=[
            col(PB_MQ), col(PB_MK), col(PB_MV), col(PF_MO),
            pl.BlockSpec((1, ln, GATE_LANES), lambda bi, ci: (bi, ci, 0)),
            pl.BlockSpec((GATE_ROWS, ln), lambda bi, ci: (0, bi * nc + ci)),
            pl.BlockSpec((1, GATE_LANES), lambda bi, ci: (0, 0)),
            pl.BlockSpec((GATE_ROWS, 1), lambda bi, ci: (0, 0)),
            pl.BlockSpec((1, w), lambda bi, ci: (0, 0)),
        ] + cast_in,
        out_specs=[pl.BlockSpec((1, ln, w), lambda bi, ci: (bi, ci, 0))] + cast_out,
        out_shape=[jax.ShapeDtypeStruct((b, s, w), BF16)] + cast_shapes,
        scratch_shapes=[pltpu.VMEM((N_HEADS, HEAD_DIM, 2 * HEAD_DIM), F32),
                        pltpu.VMEM((8, 128), F32)],
        compiler_params=_params(("arbitrary", "arbitrary")),
        name="mlstm",
    )(pb3, pb3, pb3, pf3, gates_c.reshape(b, s, GATE_LANES), gates_r, bias_c, bias_r, head_gain,
      *[arr for arr, _ in cast_jobs])
    return out[0], tuple(out[1:])


def _dil_body(q_ref, k_ref, v_ref, qg_ref, kg_ref, sl_ref, o_ref, qn, kn, m_s, l_s):
    s = q_ref.shape[1]
    n_blocks = s // BLK
    vv, o2 = v_ref.at[0], o_ref.at[0]
    qn[...] = _rms(q_ref[0], qg_ref[...]) * ATTN_SCALE
    kn[...] = _rms(k_ref[0], kg_ref[...])
    slope = sl_ref[0, 0:1, 0:1]
    qq = lax.broadcasted_iota(jnp.int32, (BLK, 2 * BLK), 0)
    kk = lax.broadcasted_iota(jnp.int32, (BLK, 2 * BLK), 1)
    dist = jnp.where(kk < BLK, qq - kk, qq - kk + 2 * BLK)
    in_window = jnp.logical_and(dist >= 0, dist <= BLK)
    prev_lanes = lax.broadcasted_iota(jnp.int32, (1, 2 * BLK), 1) >= BLK
    ones = jnp.ones((2 * BLK, HEAD_DIM), BF16)
    assert n_blocks % DIL_PAR == 0

    for pi, (window, dil) in enumerate(reversed(DILATED_PATTERNS)):
        assert window // dil == BLK and s % (dil * BLK) == 0
        nb = s // (dil * BLK)
        assert nb % DIL_PAR == 0 or DIL_PAR % nb == 0
        bias = jnp.where(in_window, (-float(dil) * slope) * dist.astype(F32), NEG_BIG)
        first, last = pi == 0, pi == len(DILATED_PATTERNS) - 1

        def rows(start, dil=dil):
            return pl.ds(start, BLK) if dil == 1 else pl.ds(start, BLK, stride=dil)

        def group(t0, carry, dil=dil, nb=nb, bias=bias, first=first, last=last, rows=rows):
            cur, qb, kc, vc, no_prev = [], [], [], [], []
            for i in range(DIL_PAR):
                t = t0 * DIL_PAR + i
                r = t // nb
                n = t - r * nb
                c = rows(r + dil * BLK * n)
                cur.append(c)
                qb.append(qn[c, :].astype(BF16))
                kc.append(kn[c, :].astype(BF16))
                vc.append(vv[c, :].astype(BF16))
                no_prev.append(jnp.where(prev_lanes, jnp.where(n > 0, 0.0, NEG_BIG), 0.0))
                if i == 0:
                    p = rows(r + dil * BLK * jnp.maximum(n - 1, 0))
                    kp, vp = [kn[p, :].astype(BF16)], [vv[p, :].astype(BF16)]
                elif i % nb == 0:
                    kp.append(kc[i])
                    vp.append(vc[i])
                else:
                    kp.append(kc[i - 1])
                    vp.append(vc[i - 1])
            k2 = [jnp.concatenate([a, b], axis=0) for a, b in zip(kc, kp)]
            v2 = [jnp.concatenate([a, b], axis=0) for a, b in zip(vc, vp)]
            sc = [jnp.maximum(_dot_nt(q, k) + bias + off, NEG_BIG) for q, k, off in zip(qb, k2, no_prev)]
            m_b = [jnp.max(x, axis=1, keepdims=True) for x in sc]
            pr = [jnp.exp(x - m).astype(BF16) for x, m in zip(sc, m_b)]
            nd = [_dot(p, jnp.concatenate([v, ones], axis=1)) for p, v in zip(pr, v2)]
            res = []
            for c, m, x in zip(cur, m_b, nd):
                num, den = x[:, :HEAD_DIM], x[:, HEAD_DIM:]
                if first:
                    res.append((jnp.broadcast_to(m, (BLK, HEAD_DIM)), den, num))
                else:
                    m_o = m_s[c, :]
                    m_n = jnp.maximum(m_o, m)
                    a_o = jnp.exp(m_o - m_n)
                    a_b = jnp.exp(m - m_n)
                    res.append((m_n, l_s[c, :] * a_o + den * a_b, o2[c, :] * a_o + num * a_b))
            for c, (m_n, l_n, acc) in zip(cur, res):
                if last:
                    o2[c, :] = acc / l_n
                else:
                    m_s[c, :] = m_n
                    l_s[c, :] = l_n
                    o2[c, :] = acc
            return carry

        lax.fori_loop(0, n_blocks // DIL_PAR, group, 0)


def _dilated(pf3, q_gain, k_gain, slopes):
    b, s, _ = pf3.shape

    def col(g):
        return pl.BlockSpec((1, s, HEAD_DIM), lambda bi, hi, g=g: (bi, 0, g * N_HEADS + hi))

    vec = pl.BlockSpec((1, HEAD_DIM), lambda bi, hi: (0, 0))
    return pl.pallas_call(
        _dil_body,
        grid=(b, N_HEADS),
        in_specs=[col(PF_CQ), col(PF_CK), col(PF_CV), vec, vec,
                  pl.BlockSpec((1, 8, HEAD_DIM), lambda bi, hi: (hi, 0, 0))],
        out_specs=pl.BlockSpec((1, s, HEAD_DIM), lambda bi, hi: (bi, 0, hi)),
        out_shape=jax.ShapeDtypeStruct((b, s, GROUP_WIDTH), F32),
        scratch_shapes=[pltpu.VMEM((s, HEAD_DIM), F32)] * 4,
        compiler_params=_params(("parallel", "parallel")),
        name="dilated",
    )(pf3, pf3, pf3, q_gain, k_gain, slopes)


def _sb_body(q_ref, k_ref, v_ref, *rest):
    n_cast = (len(rest) - 2) // 2
    cast_src, o_ref, cast_dst, z_ref = rest[:n_cast], rest[n_cast], rest[n_cast + 1:-1], rest[-1]
    for src, dst in zip(cast_src, cast_dst):
        dst[...] = src[...].astype(BF16)
    i = pl.program_id(2)
    nr, nk = SB_Q, SB_K
    per_tile = nr // nk
    qq = lax.broadcasted_iota(jnp.int32, (nr, nk), 0)
    kk = lax.broadcasted_iota(jnp.int32, (nr, nk), 1)
    ka = lax.broadcasted_iota(jnp.int32, (nk, nk), 0)
    kb = lax.broadcasted_iota(jnp.int32, (nk, nk), 1)
    after = jnp.where(ka > kb, 1.0, 0.0).astype(BF16)
    o_ref[...] = jnp.zeros_like(o_ref)

    heads = [slice(h * HEAD_DIM, (h + 1) * HEAD_DIM) for h in range(SB_HEADS)]

    def logits(j):
        keys = pl.ds(pl.multiple_of(j * nk, nk), nk)
        return [_dot_nt(q_ref[0, :, sl], k_ref[0, keys, sl]) for sl in heads]

    def step(j, gone, masked):
        keys = pl.ds(pl.multiple_of(j * nk, nk), nk)
        strict = kk < qq if per_tile == 1 else (kk + j * nk) < (qq + i * nr)
        zs = [z_ref[h] for h in range(SB_HEADS)]
        sps = [jnp.maximum(z, 0.0) + jnp.log2(1.0 + jnp.exp2(-jnp.abs(z))) for z in zs]
        log_beta = [z - sp for z, sp in zip(zs, sps)]
        drops = [jnp.where(strict, sp, 0.0) for sp in sps] if masked else sps
        gone_next = tuple(g + jnp.sum(drop, axis=1, keepdims=True) for g, drop in zip(gone, drops))
        for h, z in enumerate(logits(jnp.maximum(j - 1, 0))):
            z_ref[h] = z
        laters = [_dot(drop.astype(BF16), after) for drop in drops]
        ws = [jnp.exp2(lb - later - g) for lb, later, g in zip(log_beta, laters, gone)]
        if masked:
            ws = [jnp.where(strict, w, 0.0) for w in ws]
        pv = [_dot(w.astype(BF16), v_ref[0, keys, sl]) for w, sl in zip(ws, heads)]
        o_ref[0] += jnp.concatenate(pv, axis=1)
        return gone_next

    top = per_tile * i + per_tile - 1
    for h, z in enumerate(logits(top)):
        z_ref[h] = z
    gone = tuple(jnp.zeros((nr, 1), F32) for _ in range(SB_HEADS))
    for r in range(per_tile):
        gone = step(top - r, gone, True)
    lax.fori_loop(0, per_tile * i, lambda jj, g: step(per_tile * i - 1 - jj, g, False), gone)


def _stick_breaking(pb3, cast_jobs=()):
    b, s, _ = pb3.shape
    w = SB_HEADS * HEAD_DIM
    groups = N_HEADS // SB_HEADS
    nq = s // SB_Q
    cast_in, cast_out, cast_shapes = [], [], []
    for arr, layer in cast_jobs:
        _, rows, cols = arr.shape
        rb = _cast_rows(rows, b * groups * nq)
        step = lambda bi, gi, qi, last=rows // rb - 1: jnp.minimum((bi * groups + gi) * nq + qi, last)
        cast_in.append(pl.BlockSpec((None, rb, cols), lambda bi, gi, qi, f=step, l=layer: (l, f(bi, gi, qi), 0)))
        cast_out.append(pl.BlockSpec((rb, cols), lambda bi, gi, qi, f=step: (f(bi, gi, qi), 0)))
        cast_shapes.append(jax.ShapeDtypeStruct((rows, cols), BF16))
    out = pl.pallas_call(
        _sb_body,
        grid=(b, groups, nq),
        in_specs=[
            pl.BlockSpec((1, SB_Q, w), lambda bi, gi, qi: (bi, qi, PB_SQ * groups + gi)),
            pl.BlockSpec((1, s, w), lambda bi, gi, qi: (bi, 0, PB_SK * groups + gi)),
            pl.BlockSpec((1, s, w), lambda bi, gi, qi: (bi, 0, PB_SV * groups + gi)),
        ] + cast_in,
        out_specs=[pl.BlockSpec((1, SB_Q, w), lambda bi, gi, qi: (bi, qi, gi))] + cast_out,
        out_shape=[jax.ShapeDtypeStruct((b, s, GROUP_WIDTH), F32)] + cast_shapes,
        scratch_shapes=[pltpu.VMEM((SB_HEADS, SB_Q, SB_K), F32)],
        compiler_params=_params(("arbitrary", "arbitrary", "arbitrary")),
        name="stick_breaking",
    )(pb3, pb3, pb3, *[arr for arr, _ in cast_jobs])
    return out[0], tuple(out[1:])


def _cast_body(src, dst):
    dst[...] = src[...].astype(BF16)


def _cast_layer(stacked, layer):
    _, rows, cols = stacked.shape
    rb = _cast_rows(rows, CAST_STEPS)
    return pl.pallas_call(
        _cast_body,
        grid=(rows // rb,),
        in_specs=[pl.BlockSpec((None, rb, cols), lambda i: (layer, i, 0))],
        out_specs=pl.BlockSpec((rb, cols), lambda i: (i, 0)),
        out_shape=jax.ShapeDtypeStruct((rows, cols), BF16),
        compiler_params=_params(("parallel",)),
        name="cast_bf16",
    )(stacked)


def _outproj_body(x_ref, ya_ref, yb_ref, yc_ref, yd_ref, gc_ref, gd_ref, w_ref, o_ref):
    yc = _rms(yc_ref[...], gc_ref[...]).astype(BF16)
    yd = _rms(yd_ref[...], gd_ref[...]).astype(BF16)
    w = GROUP_WIDTH
    acc = _dot(ya_ref[...], w_ref[0:w, :])
    acc += _dot(yb_ref[...], w_ref[w:2 * w, :])
    acc += _dot(yc, w_ref[2 * w:3 * w, :])
    acc += _dot(yd, w_ref[3 * w:4 * w, :])
    o_ref[...] = x_ref[...] + acc


def _outproj(x, ya, yb, yc, yd, gain_c, gain_d, w_out, tm=512):
    m, d = x.shape
    w = GROUP_WIDTH
    yblk = pl.BlockSpec((tm, w), lambda i: (i, 0))
    vec = pl.BlockSpec((1, w), lambda i: (0, 0))
    return pl.pallas_call(
        _outproj_body,
        grid=(m // tm,),
        in_specs=[pl.BlockSpec((tm, d), lambda i: (i, 0)), yblk, yblk, yblk, yblk, vec, vec,
                  pl.BlockSpec((4 * w, d), lambda i: (0, 0))],
        out_specs=pl.BlockSpec((tm, d), lambda i: (i, 0)),
        out_shape=jax.ShapeDtypeStruct((m, d), F32),
        compiler_params=_params(("parallel",)),
        name="outproj",
    )(x, ya, yb, yc, yd, gain_c, gain_d, w_out)


def _row(v):
    return v.reshape(1, -1).astype(F32)


_W_IN_GROUP_ROWS = (0, 512, 2560, 3080, 3592, 4104, 1024, 1536, 2048, 4616, 5128, 5640)
_W_IN_GATE_ROW = 6 * GROUP_WIDTH


def _w_in_group_row(g):
    row = jnp.int32(_W_IN_GROUP_ROWS[0])
    for k in range(1, len(_W_IN_GROUP_ROWS)):
        row = jnp.where(g == k, _W_IN_GROUP_ROWS[k], row)
    return row


def _split_w_in_body(grp_ref, gate_ref, w_ref, wg_ref):
    w_ref[...] = grp_ref[...].T.astype(BF16)

    @pl.when(pl.program_id(0) == 0)
    def _():
        gates = gate_ref[...]
        pad = jnp.zeros((GATE_LANES - gates.shape[0], gates.shape[1]), F32)
        wg_ref[...] = jnp.concatenate([gates, pad], axis=0).T.astype(BF16)


def _split_w_in(w_in_t, layer):
    n_layers, d_in, d = w_in_t.shape
    n_groups = len(_W_IN_GROUP_ROWS)
    base = layer * d_in
    assert base % SUBLANES == 0 and all(r % SUBLANES == 0 for r in _W_IN_GROUP_ROWS)
    flat = w_in_t.reshape(n_layers * d_in, d)
    return pl.pallas_call(
        _split_w_in_body,
        grid=(n_groups,),
        in_specs=[pl.BlockSpec((pl.Element(GROUP_WIDTH), pl.Element(d)),
                               lambda g: (pl.multiple_of(base + _w_in_group_row(g), SUBLANES), 0)),
                  pl.BlockSpec((pl.Element(2 * N_HEADS), pl.Element(d)),
                               lambda g: (base + _W_IN_GATE_ROW, 0))],
        out_specs=[pl.BlockSpec((d, GROUP_WIDTH), lambda g: (0, g)),
                   pl.BlockSpec((d, GATE_LANES), lambda g: (0, 0))],
        out_shape=[jax.ShapeDtypeStruct((d, n_groups * GROUP_WIDTH), BF16),
                   jax.ShapeDtypeStruct((d, GATE_LANES), BF16)],
        compiler_params=_params(("arbitrary",)),
        name="split_w_in",
    )(flat, flat)


def kernel(x, ffn1_norm, ffn1_w_gate, ffn1_w_up, ffn1_w_down, mix_norm, w_in, lru_conv_w, lru_conv_b, lru_w_a, lru_b_a, lru_w_x, lru_b_x, lru_lambda, mlstm_ig_bias, mlstm_fg_bias, attn_q_gain, attn_k_gain, group_out_gain, w_out, ffn2_norm, ffn2_w_gate, ffn2_w_up, ffn2_w_down):
    b, s, d = x.shape
    depth = w_in.shape[0]
    m = b * s
    w = GROUP_WIDTH
    slopes = 2.0 ** (-8.0 * jnp.arange(1, N_HEADS + 1, dtype=F32) / N_HEADS)
    slopes = jnp.broadcast_to(slopes[:, None, None], (N_HEADS, 8, HEAD_DIM))
    col_scale = jnp.ones((6, w), F32).at[PB_MK].set(ATTN_SCALE).at[PB_SQ].set(SB_Q_SCALE).reshape(1, D_HALF)

    ffn_w = tuple(_cast_layer(t, 0) for t in (ffn1_w_gate, ffn1_w_up, ffn1_w_down))

    w_in_t = jnp.swapaxes(w_in, 1, 2)

    xf = x.reshape(m, d)
    for l in range(depth):
        xf = _ffn(xf, _row(ffn1_norm[l]), ffn_w)

        w_main, w_gate = _split_w_in(w_in_t, l)
        pf, pb, gates_c, gates_r = _inproj(xf, _row(mix_norm[l]), w_main, col_scale, w_gate)
        pf3 = pf.reshape(b, s, D_HALF)
        pb3 = pb.reshape(b, s, D_HALF)

        gains = group_out_gain[l].reshape(4, 1, w)
        ya = _lru(pf3, lru_conv_w[l], _row(lru_conv_b[l]), lru_w_a[l].astype(BF16), _row(lru_b_a[l]),
                  lru_w_x[l].astype(BF16), _row(lru_b_x[l]), _row(lru_lambda[l]), gains[0])

        gate_bias = jnp.concatenate([mlstm_ig_bias[l], mlstm_fg_bias[l]]).astype(F32)
        bias_c = jnp.pad(gate_bias, (0, GATE_LANES - 2 * N_HEADS)).reshape(1, GATE_LANES)
        bias_r = jnp.pad(gate_bias, (0, GATE_ROWS - 2 * N_HEADS)).reshape(GATE_ROWS, 1)
        jobs = [(w_out, l)] + [(t, l) for t in (ffn2_w_gate, ffn2_w_up, ffn2_w_down)]
        if l + 1 < depth:
            jobs += [(t, l + 1) for t in (ffn1_w_gate, ffn1_w_up, ffn1_w_down)]
        yb, _ = _mlstm(pf3, pb3, gates_c, gates_r, bias_c, bias_r, gains[1])

        yc = _dilated(pf3, _row(attn_q_gain[l]), _row(attn_k_gain[l]), slopes)
        yd, cast = _stick_breaking(pb3, jobs)
        w_out_b, ffn2_w, ffn_w = cast[0], cast[1:4], cast[4:]

        xf = _outproj(xf, ya.reshape(m, w), yb.reshape(m, w), yc.reshape(m, w), yd.reshape(m, w),
                      gains[2], gains[3], w_out_b)

        xf = _ffn(xf, _row(ffn2_norm[l]), ffn2_w)
    return xf.reshape(b, s, d)
```
